```python
import math
import jax, jax.numpy as jnp
from jax import lax
import numpy as np

D_MODEL = 1024
BATCH = 8
SEQ = 4096
DEPTH = 4

CHUNK = 64
Q_BLOCK = 128
EPS = 1e-6
CONV_WIDTH = D_MODEL
CONV_KERNEL = 31
GLA_HEADS = 4
GLA_DK = D_MODEL // 2 // GLA_HEADS
GLA_DV = D_MODEL // GLA_HEADS
GLA_RANK = 16
GLA_GATE_NORM = 16.0
DIFF_HEADS = 8
DIFF_HD = D_MODEL // DIFF_HEADS // 2
DIFF_VD = 2 * DIFF_HD
FFN_HIDDEN = -(-8 * D_MODEL // (3 * 256)) * 256

IN_SPLITS = (
    CONV_WIDTH, CONV_WIDTH,
    GLA_HEADS * GLA_DK, GLA_HEADS * GLA_DK,
    GLA_HEADS * GLA_DV, GLA_HEADS * GLA_DV,
    GLA_RANK,
    DIFF_HEADS * 2 * DIFF_HD, DIFF_HEADS * 2 * DIFF_HD,
    DIFF_HEADS * DIFF_VD,
    D_MODEL, D_MODEL, D_MODEL,
)
IN_OFFSETS = tuple(sum(IN_SPLITS[:i + 1]) for i in range(len(IN_SPLITS) - 1))
IN_WIDTH = sum(IN_SPLITS)

kernel_name = 'hybrid_conv_gla_diffattn_streaming'


def rms_norm(x, g):
    xf = x.astype(jnp.float32)
    y = xf * lax.rsqrt(jnp.mean(xf * xf, axis=-1, keepdims=True) + EPS)
    return (y * g.astype(jnp.float32)).astype(x.dtype)


def layer_norm(x, g, b):
    xf = x.astype(jnp.float32)
    mu = jnp.mean(xf, axis=-1, keepdims=True)
    xc = xf - mu
    y = xc * lax.rsqrt(jnp.mean(xc * xc, axis=-1, keepdims=True) + EPS)
    return (y * g.astype(jnp.float32) + b.astype(jnp.float32)).astype(x.dtype)


def conv_module(a, gate, w_dw, ln_g, ln_b, w_pw):
    u = a * jax.nn.sigmoid(gate)
    rhs = w_dw[:, None, :].astype(u.dtype)
    u = lax.conv_general_dilated(
        u, rhs, window_strides=(1,), padding=[(CONV_KERNEL - 1, 0)],
        dimension_numbers=('NWC', 'WIO', 'NWC'), feature_group_count=CONV_WIDTH)
    u = jax.nn.silu(layer_norm(u, ln_g, ln_b))
    return u @ w_pw


def gla(q, k, v, og, f_low, w_f_up, b_f, norm_g):
    bsz, seq, _ = q.shape
    n = seq // CHUNK
    dt = v.dtype
    f32 = jnp.float32
    q = q.reshape(bsz, n, CHUNK, GLA_HEADS, GLA_DK).astype(f32) * (GLA_DK ** -0.5)
    k = k.reshape(bsz, n, CHUNK, GLA_HEADS, GLA_DK).astype(f32)
    v = v.reshape(bsz, n, CHUNK, GLA_HEADS, GLA_DV).astype(f32)
    logf = jax.nn.log_sigmoid((f_low @ w_f_up + b_f).astype(f32)) / GLA_GATE_NORM
    g = jnp.cumsum(logf.reshape(bsz, n, CHUNK, GLA_HEADS, GLA_DK), axis=2)
    g_last = g[:, :, -1:]
    q_g = q * jnp.exp(g)
    k_g = k * jnp.exp(-g)
    k_d = k * jnp.exp(g_last - g)
    causal = jnp.tril(jnp.ones((CHUNK, CHUNK), dtype=bool))
    att = jnp.einsum('bnihd,bnjhd->bnhij', q_g, k_g)
    att = jnp.where(causal, att, 0.0)
    o_intra = jnp.einsum('bnhij,bnjhv->bnihv', att, v)
    u = jnp.einsum('bnjhd,bnjhv->nbhdv', k_d, v)
    decay = jnp.exp(g_last[:, :, 0]).transpose(1, 0, 2, 3)

    def step(s, inp):
        dec, un = inp
        return dec[..., None] * s + un, s

    s0 = jnp.zeros((bsz, GLA_HEADS, GLA_DK, GLA_DV), f32)
    _, s_prev = lax.scan(step, s0, (decay, u))
    o_inter = jnp.einsum('bnihd,nbhdv->bnihv', q_g, s_prev)
    o = rms_norm(o_intra + o_inter, norm_g)
    o = o.reshape(bsz, seq, GLA_HEADS * GLA_DV).astype(dt)
    return o * jax.nn.silu(og)


def diff_attention(q, k, v, lq1, lk1, lq2, lk2, norm_g, lambda_init):
    bsz, seq, _ = q.shape
    f32 = jnp.float32
    q = q.reshape(bsz, seq, DIFF_HEADS, 2, DIFF_HD)
    k = k.reshape(bsz, seq, DIFF_HEADS, 2, DIFF_HD)
    v = v.reshape(bsz, seq, DIFF_HEADS, DIFF_VD)
    lam = (jnp.exp(jnp.sum(lq1.astype(f32) * lk1.astype(f32)))
           - jnp.exp(jnp.sum(lq2.astype(f32) * lk2.astype(f32))) + lambda_init)
    nqb = seq // Q_BLOCK
    qb = q.reshape(bsz, nqb, Q_BLOCK, DIFF_HEADS, 2, DIFF_HD).transpose(1, 0, 2, 3, 4, 5)
    key_chunk = jnp.arange(seq) // CHUNK
    scale = DIFF_HD ** -0.5

    def block(args):
        qi, idx = args
        s = jnp.einsum('bqhtd,bkhtd->bhtqk', qi, k, preferred_element_type=f32) * scale
        q_chunk = (idx * Q_BLOCK + jnp.arange(Q_BLOCK)) // CHUNK
        mask = key_chunk[None, :] <= q_chunk[:, None]
        s = jnp.where(mask, s, -jnp.inf)
        p = jax.nn.softmax(s, axis=-1)
        pd = p[:, :, 0] - lam * p[:, :, 1]
        return jnp.einsum('bhqk,bkhv->bqhv', pd.astype(v.dtype), v)

    o = lax.map(block, (qb, jnp.arange(nqb)))
    o = o.transpose(1, 0, 2, 3, 4).reshape(bsz, seq, DIFF_HEADS, DIFF_VD)
    o = rms_norm(o, norm_g) * (1.0 - lambda_init)
    return o.reshape(bsz, seq, DIFF_HEADS * DIFF_VD)


def setup_inputs(seed: int = 0) -> dict:
    key = jax.random.key(seed)
    ks = jax.random.split(key, 20)
    f32 = jnp.float32
    L, D = DEPTH, D_MODEL

    def nrm(k, shape, scale):
        return jax.random.normal(k, shape, f32) * scale

    return {
        'x': nrm(ks[0], (BATCH, SEQ, D), 1.0),
        'norm_mix_g': 1.0 + nrm(ks[1], (L, D), 0.02),
        'w_in': nrm(ks[2], (L, D, IN_WIDTH), D ** -0.5),
        'conv_dw': nrm(ks[3], (L, CONV_KERNEL, CONV_WIDTH), CONV_KERNEL ** -0.5),
        'conv_ln_g': 1.0 + nrm(ks[4], (L, CONV_WIDTH), 0.02),
        'conv_ln_b': nrm(ks[5], (L, CONV_WIDTH), 0.02),
        'w_conv_out': nrm(ks[6], (L, CONV_WIDTH, D), CONV_WIDTH ** -0.5),
        'gla_wf_up': nrm(ks[7], (L, GLA_RANK, GLA_HEADS * GLA_DK), GLA_RANK ** -0.5),
        'gla_bf': nrm(ks[8], (L, GLA_HEADS * GLA_DK), 0.1),
        'gla_norm_g': 1.0 + nrm(ks[9], (L, GLA_DV), 0.02),
        'diff_lq1': nrm(ks[10], (L, DIFF_HD), 0.1),
        'diff_lk1': nrm(ks[11], (L, DIFF_HD), 0.1),
        'diff_lq2': nrm(ks[12], (L, DIFF_HD), 0.1),
        'diff_lk2': nrm(ks[13], (L, DIFF_HD), 0.1),
        'diff_norm_g': 1.0 + nrm(ks[14], (L, DIFF_VD), 0.02),
        'w_out': nrm(ks[15], (L, D, D), D ** -0.5),
        'norm_ffn_g': 1.0 + nrm(ks[16], (L, D), 0.02),
        'w_ffn_in': nrm(ks[17], (L, D, 2 * FFN_HIDDEN), D ** -0.5),
        'w_ffn_out': nrm(ks[18], (L, FFN_HIDDEN, D), FFN_HIDDEN ** -0.5),
        'final_norm_g': 1.0 + nrm(ks[19], (D,), 0.02),
    }


def reference(x, norm_mix_g, w_in, conv_dw, conv_ln_g, conv_ln_b, w_conv_out,
              gla_wf_up, gla_bf, gla_norm_g, diff_lq1, diff_lk1, diff_lq2, diff_lk2,
              diff_norm_g, w_out, norm_ffn_g, w_ffn_in, w_ffn_out, final_norm_g):
    for l in range(DEPTH):
        lambda_init = 0.8 - 0.6 * math.exp(-0.3 * l)
        h = rms_norm(x, norm_mix_g[l])
        proj = h @ w_in[l]
        (c_val, c_gate, g_q, g_k, g_v, g_og, g_f, d_q, d_k, d_v,
         m_conv, m_gla, m_diff) = jnp.split(proj, IN_OFFSETS, axis=-1)
        o_conv = conv_module(c_val, c_gate, conv_dw[l], conv_ln_g[l], conv_ln_b[l], w_conv_out[l])
        o_gla = gla(g_q, g_k, g_v, g_og, g_f, gla_wf_up[l], gla_bf[l], gla_norm_g[l])
        o_diff = diff_attention(d_q, d_k, d_v, diff_lq1[l], diff_lk1[l], diff_lq2[l], diff_lk2[l],
                                diff_norm_g[l], lambda_init)
        y = (jax.nn.sigmoid(m_conv) * o_conv + jax.nn.sigmoid(m_gla) * o_gla
             + jax.nn.sigmoid(m_diff) * o_diff)
        x = x + y @ w_out[l]
        h = rms_norm(x, norm_ffn_g[l])
        gate, up = jnp.split(h @ w_ffn_in[l], 2, axis=-1)
        x = x + (jax.nn.silu(gate) * up) @ w_ffn_out[l]
    return rms_norm(x, final_norm_g)
```

```python
import functools
import math

import jax
import jax.numpy as jnp
from jax import lax
from jax.experimental import pallas as pl
from jax.experimental.pallas import tpu as pltpu

F32 = jnp.float32
BF16 = jnp.bfloat16

EPS = 1e-6
CHUNK = 64
CONV_KERNEL = 31
CONV_HALO = 32
GLA_HEADS = 4
GLA_RANK = 16
GLA_GATE_NORM = 16.0
DIFF_HEADS = 8
LANES = 128
VMEM_LIMIT = 56 * 1024 * 1024


def _cparams(sem):
    return pltpu.CompilerParams(dimension_semantics=sem, vmem_limit_bytes=VMEM_LIMIT)


def _sigmoid(x):
    return 1.0 / (1.0 + jnp.exp(-x))


def _log_sigmoid(x):
    return jnp.minimum(x, 0.0) - jnp.log(1.0 + jnp.exp(-jnp.abs(x)))


def _split_bf16(x):
    hi = x.astype(BF16)
    lo = (x - hi.astype(F32)).astype(BF16)
    return hi, lo


def _inproj_kernel(x_ref, g_ref, w_ref, wf_ref, o_ref, f_ref, h_scr):
    @pl.when(pl.program_id(1) == 0)
    def _():
        x = x_ref[...]
        ms = jnp.mean(x * x, axis=-1, keepdims=True)
        h = (x * lax.rsqrt(ms + EPS) * g_ref[...]).astype(BF16)
        h_scr[...] = h
        f_ref[...] = jnp.dot(h, wf_ref[...], preferred_element_type=F32)

    o_ref[...] = jnp.dot(h_scr[...], w_ref[...], preferred_element_type=F32).astype(BF16)


def _inproj(x2, g, w_main, w_f, *, tm, tn):
    t, d = x2.shape
    nw = w_main.shape[1]
    return pl.pallas_call(
        _inproj_kernel,
        grid=(t // tm, nw // tn),
        in_specs=[
            pl.BlockSpec((tm, d), lambda i, j: (i, 0)),
            pl.BlockSpec((1, d), lambda i, j: (0, 0)),
            pl.BlockSpec((d, tn), lambda i, j: (0, j)),
            pl.BlockSpec((d, LANES), lambda i, j: (0, 0)),
        ],
        out_specs=[
            pl.BlockSpec((tm, tn), lambda i, j: (i, j)),
            pl.BlockSpec((tm, LANES), lambda i, j: (i, 0)),
        ],
        out_shape=[
            jax.ShapeDtypeStruct((t, nw), BF16),
            jax.ShapeDtypeStruct((t, LANES), F32),
        ],
        scratch_shapes=[pltpu.VMEM((tm, d), BF16)],
        compiler_params=_cparams(("parallel", "arbitrary")),
        name="inproj",
    )(x2, g, w_main, w_f)


def _conv_kernel(a_ref, gt_ref, m_ref, wdw_ref, lng_ref, lnb_ref, wpw_ref, o_ref, u_scr, v_scr, *, ts, rc):
    d = a_ref.shape[1]

    @pl.when(pl.program_id(1) == 0)
    def _():
        u_scr[0:CONV_HALO, :] = jnp.zeros((CONV_HALO, d), F32)

    def glu_body(r, carry):
        r0 = pl.multiple_of(r * rc, rc)
        a = a_ref[pl.ds(r0, rc), :].astype(F32)
        gt = gt_ref[pl.ds(r0, rc), :].astype(F32)
        u_scr[pl.ds(CONV_HALO + r0, rc), :] = a * _sigmoid(gt)
        return carry

    lax.fori_loop(0, ts // rc, glu_body, 0)

    lng = lng_ref[...]
    lnb = lnb_ref[...]
    first = CONV_HALO - (CONV_KERNEL - 1)

    for r in range(ts // rc):
        r0 = r * rc
        acc = jnp.zeros((rc, d), F32)
        for k in range(CONV_KERNEL):
            acc = acc + u_scr[r0 + first + k:r0 + first + k + rc, :] * wdw_ref[k:k + 1, :]
        mu = jnp.mean(acc, axis=-1, keepdims=True)
        xc = acc - mu
        var = jnp.mean(xc * xc, axis=-1, keepdims=True)
        y = xc * lax.rsqrt(var + EPS) * lng + lnb
        v_scr[r0:r0 + rc, :] = (y * _sigmoid(y)).astype(BF16)

    u_scr[0:CONV_HALO, :] = u_scr[ts:ts + CONV_HALO, :]

    o = jnp.dot(v_scr[...], wpw_ref[...], preferred_element_type=F32)
    o_ref[...] = (o * _sigmoid(m_ref[...].astype(F32))).astype(BF16)


def _conv(proj, wdw, lng, lnb, wpw, *, bsz, seq, ts, rc, blk):
    d = wpw.shape[0]
    nst = seq // ts
    kern = functools.partial(_conv_kernel, ts=ts, rc=rc)

    def col(c):
        return pl.BlockSpec((ts, d), lambda b, s: (b * nst + s, c))

    def full(shape):
        return pl.BlockSpec(shape, lambda b, s: (0, 0))

    return pl.pallas_call(
        kern,
        grid=(bsz, nst),
        in_specs=[col(blk["c_val"]), col(blk["c_gate"]), col(blk["m_conv"]),
                  full(wdw.shape), full(lng.shape), full(lnb.shape), full(wpw.shape)],
        out_specs=pl.BlockSpec((ts, d), lambda b, s: (b * nst + s, 0)),
        out_shape=jax.ShapeDtypeStruct((bsz * seq, d), BF16),
        scratch_shapes=[pltpu.VMEM((CONV_HALO + ts, d), F32), pltpu.VMEM((ts, d), BF16)],
        compiler_params=_cparams(("parallel", "arbitrary")),
        name="conv",
    )(proj, proj, proj, wdw, lng, lnb, wpw)


def _gla_kernel(q_ref, k_ref, v_ref, og_ref, m_ref, f_ref, wf_ref, bf_ref, ng_ref, o_ref, st_scr, *, tg, dk, dv):
    @pl.when(pl.program_id(1) == 0)
    def _():
        st_scr[...] = jnp.zeros(st_scr.shape, F32)

    row = lax.broadcasted_iota(jnp.int32, (CHUNK, CHUNK), 0)
    col = lax.broadcasted_iota(jnp.int32, (CHUNK, CHUNK), 1)
    causal = row >= col
    tril = jnp.where(causal, 1.0, 0.0).astype(BF16)
    wf = wf_ref[...]
    bf = bf_ref[...]
    ng = ng_ref[...]
    qscale = dk ** -0.5

    def chunk(c, carry):
        r0 = pl.multiple_of(c * CHUNK, CHUNK)
        rows = pl.ds(r0, CHUNK)
        f_hi, f_lo = _split_bf16(f_ref[rows, :])
        logit = (jnp.dot(f_hi, wf, preferred_element_type=F32)
                 + jnp.dot(f_lo, wf, preferred_element_type=F32) + bf)
        logf = _log_sigmoid(logit) * (1.0 / GLA_GATE_NORM)
        l_hi, l_lo = _split_bf16(logf)
        g = (jnp.dot(tril, l_hi, preferred_element_type=F32)
             + jnp.dot(tril, l_lo, preferred_element_type=F32))
        g_last = g[CHUNK - 1:CHUNK, :]
        q = q_ref[rows, :].astype(F32) * qscale
        k = k_ref[rows, :].astype(F32)
        q_g = (q * jnp.exp(g)).astype(BF16)
        k_g = (k * jnp.exp(-g)).astype(BF16)
        k_d = (k * jnp.exp(g_last - g)).astype(BF16)
        decay = jnp.exp(g_last)
        for h in range(GLA_HEADS):
            ks = slice(h * dk, (h + 1) * dk)
            vs = slice(h * dv, (h + 1) * dv)
            v = v_ref[rows, vs]
            att = lax.dot_general(q_g[:, ks], k_g[:, ks], (((1,), (1,)), ((), ())),
                                  preferred_element_type=F32)
            att = jnp.where(causal, att, 0.0).astype(BF16)
            st = st_scr[h]
            o = (jnp.dot(att, v, preferred_element_type=F32)
                 + lax.dot_general(q_g[:, ks], st.astype(BF16), (((1,), (1,)), ((), ())),
                                   preferred_element_type=F32))
            st_scr[h] = st * decay[:, ks] + lax.dot_general(
                v, k_d[:, ks], (((0,), (0,)), ((), ())), preferred_element_type=F32)
            o = o * lax.rsqrt(jnp.mean(o * o, axis=-1, keepdims=True) + EPS) * ng
            og = og_ref[rows, vs].astype(F32)
            gate = _sigmoid(m_ref[rows, vs].astype(F32))
            o_ref[rows, vs] = (gate * (o * (og * _sigmoid(og)))).astype(BF16)
        return carry

    lax.fori_loop(0, tg // CHUNK, chunk, 0)


def _gla(proj, f_low, wf_up, bf, ng, *, bsz, seq, tg, blk):
    hdk = wf_up.shape[1]
    dk = hdk // GLA_HEADS
    dv = ng.shape[1]
    hdv = GLA_HEADS * dv
    nst = seq // tg
    kern = functools.partial(_gla_kernel, tg=tg, dk=dk, dv=dv)

    def col(width, c):
        return pl.BlockSpec((tg, width), lambda b, s: (b * nst + s, c))

    def full(shape):
        return pl.BlockSpec(shape, lambda b, s: (0, 0))

    return pl.pallas_call(
        kern,
        grid=(bsz, nst),
        in_specs=[col(hdk, blk["g_q"]), col(hdk, blk["g_k"]), col(hdv, blk["g_v"]), col(hdv, blk["g_og"]),
                  col(hdv, blk["m_gla"]), col(LANES, 0),
                  full(wf_up.shape), full(bf.shape), full(ng.shape)],
        out_specs=pl.BlockSpec((tg, hdv), lambda b, s: (b * nst + s, 0)),
        out_shape=jax.ShapeDtypeStruct((bsz * seq, hdv), BF16),
        scratch_shapes=[pltpu.VMEM((GLA_HEADS, dv, dk), F32)],
        compiler_params=_cparams(("parallel", "arbitrary")),
        name="gla",
    )(proj, proj, proj, proj, proj, f_low, wf_up, bf, ng)


def _diff_kernel(lam_ref, q_ref, k_ref, v_ref, m_ref, ng_ref, o_ref, m_scr, l_scr, acc_scr, *, tq, hd, lambda_init):
    qi = pl.program_id(2)
    q = q_ref[...]
    lane = lax.broadcasted_iota(jnp.int32, q.shape, 1)
    zero = jnp.zeros_like(q)
    scale = hd ** -0.5
    qq = jnp.concatenate([jnp.where(lane < hd, q, zero), jnp.where(lane >= hd, q, zero)], axis=0)
    qq = (qq.astype(F32) * scale).astype(BF16)

    m_scr[...] = jnp.full(m_scr.shape, -jnp.inf, F32)
    l_scr[...] = jnp.zeros(l_scr.shape, F32)
    acc_scr[...] = jnp.zeros(acc_scr.shape, F32)

    def step(j, masked):
        c0 = pl.multiple_of(j * tq, tq)
        k = k_ref[pl.ds(c0, tq), :]
        v = v_ref[pl.ds(c0, tq), :]
        s = lax.dot_general(qq, k, (((1,), (1,)), ((), ())), preferred_element_type=F32)
        if masked:
            qc = lax.broadcasted_iota(jnp.int32, s.shape, 0) % tq // CHUNK
            kc = lax.broadcasted_iota(jnp.int32, s.shape, 1) // CHUNK
            s = jnp.where(kc <= qc, s, -jnp.inf)
        m_prev = m_scr[...]
        m_new = jnp.maximum(m_prev, jnp.max(s, axis=-1, keepdims=True))
        alpha = jnp.exp(m_prev - m_new)
        p = jnp.exp(s - m_new[:, 0:1])
        l_scr[...] = alpha * l_scr[...] + jnp.sum(p, axis=-1, keepdims=True)
        acc_scr[...] = alpha * acc_scr[...] + jnp.dot(p.astype(BF16), v, preferred_element_type=F32)
        m_scr[...] = m_new

    def body(j, carry):
        step(j, False)
        return carry

    lax.fori_loop(0, qi, body, 0)
    step(qi, True)

    lam = lam_ref[0]
    o = acc_scr[...] / l_scr[...]
    od = o[0:tq, :] - lam * o[tq:2 * tq, :]
    od = od * lax.rsqrt(jnp.mean(od * od, axis=-1, keepdims=True) + EPS) * ng_ref[...] * (1.0 - lambda_init)
    o_ref[...] = (od * _sigmoid(m_ref[...].astype(F32))).astype(BF16)


def _diff(lam, proj, ng, *, bsz, seq, tq, lambda_init, blk):
    vd = ng.shape[1]
    hd = vd // 2
    nq = seq // tq
    kern = functools.partial(_diff_kernel, tq=tq, hd=hd, lambda_init=lambda_init)
    return pl.pallas_call(
        kern,
        grid=(bsz, DIFF_HEADS, nq),
        in_specs=[
            pl.BlockSpec(memory_space=pltpu.SMEM),
            pl.BlockSpec((tq, vd), lambda b, h, i: (b * nq + i, blk["d_q"] + h)),
            pl.BlockSpec((seq, vd), lambda b, h, i: (b, blk["d_k"] + h)),
            pl.BlockSpec((seq, vd), lambda b, h, i: (b, blk["d_v"] + h)),
            pl.BlockSpec((tq, vd), lambda b, h, i: (b * nq + i, blk["m_diff"] + h)),
            pl.BlockSpec((1, vd), lambda b, h, i: (0, 0)),
        ],
        out_specs=pl.BlockSpec((tq, vd), lambda b, h, i: (b * nq + i, h)),
        out_shape=jax.ShapeDtypeStruct((bsz * seq, DIFF_HEADS * vd), BF16),
        scratch_shapes=[pltpu.VMEM((2 * tq, LANES), F32), pltpu.VMEM((2 * tq, LANES), F32),
                        pltpu.VMEM((2 * tq, vd), F32)],
        compiler_params=_cparams(("parallel", "parallel", "arbitrary")),
        name="diff",
    )(lam, proj, proj, proj, proj, ng)


def _lambda_kernel(p_ref, o_ref, *, lambda_init):
    p = p_ref[...]
    s1 = jnp.sum(p[0:1, :] * p[1:2, :], axis=-1, keepdims=True)
    s2 = jnp.sum(p[2:3, :] * p[3:4, :], axis=-1, keepdims=True)
    o_ref[...] = jnp.exp(s1) - jnp.exp(s2) + lambda_init


def _diff_lambda(lq1, lk1, lq2, lk2, lambda_init):
    p = jnp.stack([lq1, lk1, lq2, lk2]).astype(F32)
    out = pl.pallas_call(
        functools.partial(_lambda_kernel, lambda_init=lambda_init),
        out_shape=jax.ShapeDtypeStruct((1, 1), F32),
        name="diff_lambda",
    )(p)
    return out.reshape(1)


def _mix_ffn_kernel(x_ref, yc_ref, yg_ref, yd_ref, wo_ref, g_ref, wg_ref, wu_ref, wd_ref, fg_ref, o_ref,
                    x1_scr, h_scr, acc_scr, *, final):
    c = pl.program_id(1)

    @pl.when(c == 0)
    def _():
        y = (yc_ref[...].astype(F32) + yg_ref[...].astype(F32) + yd_ref[...].astype(F32)).astype(BF16)
        x1 = x_ref[...] + jnp.dot(y, wo_ref[...], preferred_element_type=F32)
        x1_scr[...] = x1
        ms = jnp.mean(x1 * x1, axis=-1, keepdims=True)
        h_scr[...] = (x1 * lax.rsqrt(ms + EPS) * g_ref[...]).astype(BF16)
        acc_scr[...] = jnp.zeros(acc_scr.shape, F32)

    h = h_scr[...]
    gate = jnp.dot(h, wg_ref[...], preferred_element_type=F32)
    up = jnp.dot(h, wu_ref[...], preferred_element_type=F32)
    act = (gate * _sigmoid(gate) * up).astype(BF16)
    acc_scr[...] += jnp.dot(act, wd_ref[...], preferred_element_type=F32)

    @pl.when(c == pl.num_programs(1) - 1)
    def _():
        x2 = x1_scr[...] + acc_scr[...]
        if final:
            ms = jnp.mean(x2 * x2, axis=-1, keepdims=True)
            x2 = x2 * lax.rsqrt(ms + EPS) * fg_ref[...]
        o_ref[...] = x2


def _mix_ffn(x2, yc, yg, yd, w_out, g, w_ffn_in, w_ffn_out, fg, *, tm, th, final):
    t, d = x2.shape
    hidden = w_ffn_out.shape[0]
    nh = hidden // th
    kern = functools.partial(_mix_ffn_kernel, final=final)

    def tok(dtype_unused=None):
        return pl.BlockSpec((tm, d), lambda i, c: (i, 0))

    def full(shape):
        return pl.BlockSpec(shape, lambda i, c: (0, 0))

    return pl.pallas_call(
        kern,
        grid=(t // tm, nh),
        in_specs=[tok(), tok(), tok(), tok(), full(w_out.shape), full(g.shape),
                  pl.BlockSpec((d, th), lambda i, c: (0, c)),
                  pl.BlockSpec((d, th), lambda i, c: (0, c + nh)),
                  pl.BlockSpec((th, d), lambda i, c: (c, 0)),
                  full(fg.shape)],
        out_specs=pl.BlockSpec((tm, d), lambda i, c: (i, 0)),
        out_shape=jax.ShapeDtypeStruct((t, d), F32),
        scratch_shapes=[pltpu.VMEM((tm, d), F32), pltpu.VMEM((tm, d), BF16), pltpu.VMEM((tm, d), F32)],
        compiler_params=_cparams(("parallel", "arbitrary")),
        name="mix_ffn",
    )(x2, yc, yg, yd, w_out, g, w_ffn_in, w_ffn_in, w_ffn_out, fg)


def _regroup_in_proj(w, d, hdk, rank):
    widths = (d, d, hdk, hdk, d, d, rank, d, d, d, d, d, d)
    names = ("c_val", "c_gate", "g_q", "g_k", "g_v", "g_og", "g_f", "d_q", "d_k", "d_v", "m_conv", "m_gla", "m_diff")
    offs = [0]
    for wd in widths:
        offs.append(offs[-1] + wd)
    part = {n: w[:, offs[i]:offs[i + 1]] for i, n in enumerate(names)}
    order = ("c_val", "c_gate", "m_conv", "g_v", "g_og", "m_gla", "d_v", "m_diff", "d_q", "d_k", "g_q", "g_k")
    w_main = jnp.concatenate([part[n] for n in order], axis=1).astype(BF16)
    w_f = jnp.pad(part["g_f"], ((0, 0), (0, LANES - rank))).astype(BF16)
    return w_main, w_f


def kernel(x, norm_mix_g, w_in, conv_dw, conv_ln_g, conv_ln_b, w_conv_out, gla_wf_up, gla_bf, gla_norm_g,
           diff_lq1, diff_lk1, diff_lq2, diff_lk2, diff_norm_g, w_out, norm_ffn_g, w_ffn_in, w_ffn_out,
           final_norm_g):
    bsz, seq, d = x.shape
    depth = w_in.shape[0]
    hdk = gla_wf_up.shape[2]
    rank = gla_wf_up.shape[1]
    t = bsz * seq

    blk = {"c_val": 0, "c_gate": 1, "m_conv": 2, "g_v": 3, "g_og": 4, "m_gla": 5,
           "g_q": 10 * d // hdk, "g_k": 10 * d // hdk + 1}
    vd = diff_norm_g.shape[1]
    per = d // vd
    blk.update({"d_v": 6 * per, "m_diff": 7 * per, "d_q": 8 * per, "d_k": 9 * per})

    x2 = x.reshape(t, d)
    for l in range(depth):
        lambda_init = 0.8 - 0.6 * math.exp(-0.3 * l)
        w_main, w_f = _regroup_in_proj(w_in[l], d, hdk, rank)
        proj, f_low = _inproj(x2, norm_mix_g[l][None, :], w_main, w_f, tm=min(1024, t), tn=1024)

        y_conv = _conv(proj, conv_dw[l], conv_ln_g[l][None, :], conv_ln_b[l][None, :],
                       w_conv_out[l].astype(BF16), bsz=bsz, seq=seq, ts=min(512, seq), rc=32, blk=blk)

        wf_up = jnp.pad(gla_wf_up[l], ((0, LANES - rank), (0, 0))).astype(BF16)
        y_gla = _gla(proj, f_low, wf_up, gla_bf[l][None, :], gla_norm_g[l][None, :],
                     bsz=bsz, seq=seq, tg=min(512, seq), blk=blk)

        lam = _diff_lambda(diff_lq1[l], diff_lk1[l], diff_lq2[l], diff_lk2[l], lambda_init)
        y_diff = _diff(lam, proj, diff_norm_g[l][None, :], bsz=bsz, seq=seq, tq=min(256, seq),
                       lambda_init=lambda_init, blk=blk)

        x2 = _mix_ffn(x2, y_conv, y_gla, y_diff, w_out[l].astype(BF16), norm_ffn_g[l][None, :],
                      w_ffn_in[l].astype(BF16), w_ffn_out[l].astype(BF16), final_norm_g[None, :],
                      tm=min(1024, t), th=256, final=(l == depth - 1))
    return x2.reshape(bsz, seq, d)
```

```python
import functools
import math

import jax
import jax.numpy as jnp
from jax import lax
from jax.experimental import pallas as pl
from jax.experimental.pallas import tpu as pltpu

F32 = jnp.float32
BF16 = jnp.bfloat16

EPS = 1e-6
CHUNK = 64
CONV_KERNEL = 31
CONV_HALO = 32
GLA_HEADS = 4
GLA_RANK = 16
GLA_GATE_NORM = 16.0
DIFF_HEADS = 8
LANES = 128
VMEM_LIMIT = 56 * 1024 * 1024


def _cparams(sem):
    return pltpu.CompilerParams(dimension_semantics=sem, vmem_limit_bytes=VMEM_LIMIT)


def _sigmoid(x):
    return 1.0 / (1.0 + jnp.exp(-x))


def _log_sigmoid(x):
    return jnp.minimum(x, 0.0) - jnp.log(1.0 + jnp.exp(-jnp.abs(x)))


def _split_bf16(x):
    hi = x.astype(BF16)
    lo = (x - hi.astype(F32)).astype(BF16)
    return hi, lo


def _inproj_kernel(x_ref, g_ref, w_ref, wf_ref, o_ref, f_ref, h_scr):
    @pl.when(pl.program_id(1) == 0)
    def _():
        x = x_ref[...]
        ms = jnp.mean(x * x, axis=-1, keepdims=True)
        h = (x * lax.rsqrt(ms + EPS) * g_ref[...]).astype(BF16)
        h_scr[...] = h
        f_ref[...] = jnp.dot(h, wf_ref[...], preferred_element_type=F32)

    o_ref[...] = jnp.dot(h_scr[...], w_ref[...], preferred_element_type=F32).astype(BF16)


def _inproj(x2, g, w_main, w_f, *, tm, tn):
    t, d = x2.shape
    nw = w_main.shape[1]
    return pl.pallas_call(
        _inproj_kernel,
        grid=(t // tm, nw // tn),
        in_specs=[
            pl.BlockSpec((tm, d), lambda i, j: (i, 0)),
            pl.BlockSpec((1, d), lambda i, j: (0, 0)),
            pl.BlockSpec((d, tn), lambda i, j: (0, j)),
            pl.BlockSpec((d, LANES), lambda i, j: (0, 0)),
        ],
        out_specs=[
            pl.BlockSpec((tm, tn), lambda i, j: (i, j)),
            pl.BlockSpec((tm, LANES), lambda i, j: (i, 0)),
        ],
        out_shape=[
            jax.ShapeDtypeStruct((t, nw), BF16),
            jax.ShapeDtypeStruct((t, LANES), F32),
        ],
        scratch_shapes=[pltpu.VMEM((tm, d), BF16)],
        compiler_params=_cparams(("parallel", "arbitrary")),
        name="inproj",
    )(x2, g, w_main, w_f)


def _conv_kernel(a_ref, gt_ref, m_ref, wdw_ref, lng_ref, lnb_ref, wpw_ref, o_ref, u_scr, v_scr, *, ts, rc):
    d = a_ref.shape[1]

    @pl.when(pl.program_id(1) == 0)
    def _():
        u_scr[0:CONV_HALO, :] = jnp.zeros((CONV_HALO, d), F32)

    def glu_body(r, carry):
        r0 = pl.multiple_of(r * rc, rc)
        a = a_ref[pl.ds(r0, rc), :].astype(F32)
        gt = gt_ref[pl.ds(r0, rc), :].astype(F32)
        u_scr[pl.ds(CONV_HALO + r0, rc), :] = a * _sigmoid(gt)
        return carry

    lax.fori_loop(0, ts // rc, glu_body, 0)

    lng = lng_ref[...]
    lnb = lnb_ref[...]
    first = CONV_HALO - (CONV_KERNEL - 1)

    for r in range(ts // rc):
        r0 = r * rc
        acc = jnp.zeros((rc, d), F32)
        for k in range(CONV_KERNEL):
            acc = acc + u_scr[r0 + first + k:r0 + first + k + rc, :] * wdw_ref[k:k + 1, :]
        mu = jnp.mean(acc, axis=-1, keepdims=True)
        xc = acc - mu
        var = jnp.mean(xc * xc, axis=-1, keepdims=True)
        y = xc * lax.rsqrt(var + EPS) * lng + lnb
        v_scr[r0:r0 + rc, :] = (y * _sigmoid(y)).astype(BF16)

    u_scr[0:CONV_HALO, :] = u_scr[ts:ts + CONV_HALO, :]

    o = jnp.dot(v_scr[...], wpw_ref[...], preferred_element_type=F32)
    o_ref[...] = (o * _sigmoid(m_ref[...].astype(F32))).astype(BF16)


def _conv(proj, wdw, lng, lnb, wpw, *, bsz, seq, ts, rc, blk):
    d = wpw.shape[0]
    nst = seq // ts
    kern = functools.partial(_conv_kernel, ts=ts, rc=rc)

    def col(c):
        return pl.BlockSpec((ts, d), lambda b, s: (b * nst + s, c))

    def full(shape):
        return pl.BlockSpec(shape, lambda b, s: (0, 0))

    return pl.pallas_call(
        kern,
        grid=(bsz, nst),
        in_specs=[col(blk["c_val"]), col(blk["c_gate"]), col(blk["m_conv"]),
                  full(wdw.shape), full(lng.shape), full(lnb.shape), full(wpw.shape)],
        out_specs=pl.BlockSpec((ts, d), lambda b, s: (b * nst + s, 0)),
        out_shape=jax.ShapeDtypeStruct((bsz * seq, d), BF16),
        scratch_shapes=[pltpu.VMEM((CONV_HALO + ts, d), F32), pltpu.VMEM((ts, d), BF16)],
        compiler_params=_cparams(("parallel", "arbitrary")),
        name="conv",
    )(proj, proj, proj, wdw, lng, lnb, wpw)


def _gla_kernel(q_ref, k_ref, v_ref, og_ref, m_ref, f_ref, wf_ref, bf_ref, ng_ref, o_ref, st_scr, *, tg, dk, dv):
    @pl.when(pl.program_id(1) == 0)
    def _():
        st_scr[...] = jnp.zeros(st_scr.shape, F32)

    row = lax.broadcasted_iota(jnp.int32, (CHUNK, CHUNK), 0)
    col = lax.broadcasted_iota(jnp.int32, (CHUNK, CHUNK), 1)
    causal = row >= col
    tril = jnp.where(causal, 1.0, 0.0).astype(BF16)
    wf = wf_ref[...]
    bf = bf_ref[...]
    ng = ng_ref[...]
    qscale = dk ** -0.5

    def chunk(c, carry):
        r0 = pl.multiple_of(c * CHUNK, CHUNK)
        rows = pl.ds(r0, CHUNK)
        f_hi, f_lo = _split_bf16(f_ref[rows, :])
        logit = (jnp.dot(f_hi, wf, preferred_element_type=F32)
                 + jnp.dot(f_lo, wf, preferred_element_type=F32) + bf)
        logf = _log_sigmoid(logit) * (1.0 / GLA_GATE_NORM)
        l_hi, l_lo = _split_bf16(logf)
        g = (jnp.dot(tril, l_hi, preferred_element_type=F32)
             + jnp.dot(tril, l_lo, preferred_element_type=F32))
        g_last = g[CHUNK - 1:CHUNK, :]
        q = q_ref[rows, :].astype(F32) * qscale
        k = k_ref[rows, :].astype(F32)
        q_g = (q * jnp.exp(g)).astype(BF16)
        k_g = (k * jnp.exp(-g)).astype(BF16)
        k_d = (k * jnp.exp(g_last - g)).astype(BF16)
        decay = jnp.exp(g_last)
        for h in range(GLA_HEADS):
            ks = slice(h * dk, (h + 1) * dk)
            vs = slice(h * dv, (h + 1) * dv)
            v = v_ref[rows, vs]
            att = lax.dot_general(q_g[:, ks], k_g[:, ks], (((1,), (1,)), ((), ())),
                                  preferred_element_type=F32)
            att = jnp.where(causal, att, 0.0).astype(BF16)
            st = st_scr[h]
            o = (jnp.dot(att, v, preferred_element_type=F32)
                 + lax.dot_general(q_g[:, ks], st.astype(BF16), (((1,), (1,)), ((), ())),
                                   preferred_element_type=F32))
            st_scr[h] = st * decay[:, ks] + lax.dot_general(
                v, k_d[:, ks], (((0,), (0,)), ((), ())), preferred_element_type=F32)
            o = o * lax.rsqrt(jnp.mean(o * o, axis=-1, keepdims=True) + EPS) * ng
            og = og_ref[rows, vs].astype(F32)
            gate = _sigmoid(m_ref[rows, vs].astype(F32))
            o_ref[rows, vs] = (gate * (o * (og * _sigmoid(og)))).astype(BF16)
        return carry

    lax.fori_loop(0, tg // CHUNK, chunk, 0)


def _gla(proj, f_low, wf_up, bf, ng, *, bsz, seq, tg, blk):
    hdk = wf_up.shape[1]
    dk = hdk // GLA_HEADS
    dv = ng.shape[1]
    hdv = GLA_HEADS * dv
    nst = seq // tg
    kern = functools.partial(_gla_kernel, tg=tg, dk=dk, dv=dv)

    def col(width, c):
        return pl.BlockSpec((tg, width), lambda b, s: (b * nst + s, c))

    def full(shape):
        return pl.BlockSpec(shape, lambda b, s: (0, 0))

    return pl.pallas_call(
        kern,
        grid=(bsz, nst),
        in_specs=[col(hdk, blk["g_q"]), col(hdk, blk["g_k"]), col(hdv, blk["g_v"]), col(hdv, blk["g_og"]),
                  col(hdv, blk["m_gla"]), col(LANES, 0),
                  full(wf_up.shape), full(bf.shape), full(ng.shape)],
        out_specs=pl.BlockSpec((tg, hdv), lambda b, s: (b * nst + s, 0)),
        out_shape=jax.ShapeDtypeStruct((bsz * seq, hdv), BF16),
        scratch_shapes=[pltpu.VMEM((GLA_HEADS, dv, dk), F32)],
        compiler_params=_cparams(("parallel", "arbitrary")),
        name="gla",
    )(proj, proj, proj, proj, proj, f_low, wf_up, bf, ng)


def _diff_kernel(lam_ref, q_ref, k_ref, v_ref, m_ref, ng_ref, o_ref, vt_scr, m_scr, l_scr, acc_scr,
                 *, tq, hd, lambda_init):
    qi = pl.program_id(2)
    seq = k_ref.shape[0]

    @pl.when(qi == 0)
    def _():
        for c in range(seq // tq):
            rows = slice(c * tq, (c + 1) * tq)
            vt_scr[:, rows] = v_ref[rows, :].astype(F32).T.astype(BF16)

    q = q_ref[...]
    lane = lax.broadcasted_iota(jnp.int32, q.shape, 1)
    zero = jnp.zeros_like(q)
    scale = hd ** -0.5
    qq = jnp.concatenate([jnp.where(lane < hd, q, zero), jnp.where(lane >= hd, q, zero)], axis=0)
    qq = (qq.astype(F32) * scale).astype(BF16)

    m_scr[...] = jnp.full(m_scr.shape, -jnp.inf, F32)
    l_scr[...] = jnp.zeros(l_scr.shape, F32)
    acc_scr[...] = jnp.zeros(acc_scr.shape, F32)

    def step(j, masked):
        c0 = pl.multiple_of(j * tq, tq)
        k = k_ref[pl.ds(c0, tq), :]
        vt = vt_scr[:, pl.ds(c0, tq)]
        st = lax.dot_general(k, qq, (((1,), (1,)), ((), ())), preferred_element_type=F32)
        if masked:
            kc = lax.broadcasted_iota(jnp.int32, st.shape, 0) // CHUNK
            qc = lax.broadcasted_iota(jnp.int32, st.shape, 1) % tq // CHUNK
            st = jnp.where(kc <= qc, st, -jnp.inf)
        m_prev = m_scr[...]
        m_new = jnp.maximum(m_prev, jnp.max(st, axis=0, keepdims=True))
        alpha = jnp.exp(m_prev - m_new)
        pt = jnp.exp(st - m_new)
        l_scr[...] = alpha * l_scr[...] + jnp.sum(pt, axis=0, keepdims=True)
        acc_scr[...] = alpha * acc_scr[...] + jnp.dot(vt, pt.astype(BF16), preferred_element_type=F32)
        m_scr[...] = m_new

    def body(j, carry):
        step(j, False)
        return carry

    lax.fori_loop(0, qi, body, 0)
    step(qi, True)

    lam = lam_ref[0]
    ot = acc_scr[...] / l_scr[...]
    od = (ot[:, 0:tq] - lam * ot[:, tq:2 * tq]).T
    od = od * lax.rsqrt(jnp.mean(od * od, axis=-1, keepdims=True) + EPS) * ng_ref[...] * (1.0 - lambda_init)
    o_ref[...] = (od * _sigmoid(m_ref[...].astype(F32))).astype(BF16)


def _diff(lam, proj, ng, *, bsz, seq, tq, lambda_init, blk):
    vd = ng.shape[1]
    hd = vd // 2
    nq = seq // tq
    kern = functools.partial(_diff_kernel, tq=tq, hd=hd, lambda_init=lambda_init)
    return pl.pallas_call(
        kern,
        grid=(bsz, DIFF_HEADS, nq),
        in_specs=[
            pl.BlockSpec(memory_space=pltpu.SMEM),
            pl.BlockSpec((tq, vd), lambda b, h, i: (b * nq + i, blk["d_q"] + h)),
            pl.BlockSpec((seq, vd), lambda b, h, i: (b, blk["d_k"] + h)),
            pl.BlockSpec((seq, vd), lambda b, h, i: (b, blk["d_v"] + h)),
            pl.BlockSpec((tq, vd), lambda b, h, i: (b * nq + i, blk["m_diff"] + h)),
            pl.BlockSpec((1, vd), lambda b, h, i: (0, 0)),
        ],
        out_specs=pl.BlockSpec((tq, vd), lambda b, h, i: (b * nq + i, h)),
        out_shape=jax.ShapeDtypeStruct((bsz * seq, DIFF_HEADS * vd), BF16),
        scratch_shapes=[pltpu.VMEM((vd, seq), BF16), pltpu.VMEM((1, 2 * tq), F32), pltpu.VMEM((1, 2 * tq), F32),
                        pltpu.VMEM((vd, 2 * tq), F32)],
        compiler_params=_cparams(("parallel", "parallel", "arbitrary")),
        name="diff",
    )(lam, proj, proj, proj, proj, ng)


def _lambda_kernel(p_ref, o_ref, *, lambda_init):
    p = p_ref[...]
    s1 = jnp.sum(p[0:1, :] * p[1:2, :], axis=-1, keepdims=True)
    s2 = jnp.sum(p[2:3, :] * p[3:4, :], axis=-1, keepdims=True)
    o_ref[...] = jnp.exp(s1) - jnp.exp(s2) + lambda_init


def _diff_lambda(lq1, lk1, lq2, lk2, lambda_init):
    p = jnp.stack([lq1, lk1, lq2, lk2]).astype(F32)
    out = pl.pallas_call(
        functools.partial(_lambda_kernel, lambda_init=lambda_init),
        out_shape=jax.ShapeDtypeStruct((1, 1), F32),
        name="diff_lambda",
    )(p)
    return out.reshape(1)


def _mix_ffn_kernel(x_ref, yc_ref, yg_ref, yd_ref, wo_ref, g_ref, wg_ref, wu_ref, wd_ref, fg_ref, o_ref,
                    x1_scr, h_scr, acc_scr, *, final):
    c = pl.program_id(1)

    @pl.when(c == 0)
    def _():
        y = (yc_ref[...].astype(F32) + yg_ref[...].astype(F32) + yd_ref[...].astype(F32)).astype(BF16)
        x1 = x_ref[...] + jnp.dot(y, wo_ref[...], preferred_element_type=F32)
        x1_scr[...] = x1
        ms = jnp.mean(x1 * x1, axis=-1, keepdims=True)
        h_scr[...] = (x1 * lax.rsqrt(ms + EPS) * g_ref[...]).astype(BF16)
        acc_scr[...] = jnp.zeros(acc_scr.shape, F32)

    h = h_scr[...]
    gate = jnp.dot(h, wg_ref[...], preferred_element_type=F32)
    up = jnp.dot(h, wu_ref[...], preferred_element_type=F32)
    act = (gate * _sigmoid(gate) * up).astype(BF16)
    acc_scr[...] += jnp.dot(act, wd_ref[...], preferred_element_type=F32)

    @pl.when(c == pl.num_programs(1) - 1)
    def _():
        x2 = x1_scr[...] + acc_scr[...]
        if final:
            ms = jnp.mean(x2 * x2, axis=-1, keepdims=True)
            x2 = x2 * lax.rsqrt(ms + EPS) * fg_ref[...]
        o_ref[...] = x2


def _mix_ffn(x2, yc, yg, yd, w_out, g, w_ffn_in, w_ffn_out, fg, *, tm, th, final):
    t, d = x2.shape
    hidden = w_ffn_out.shape[0]
    nh = hidden // th
    kern = functools.partial(_mix_ffn_kernel, final=final)

    def tok(dtype_unused=None):
        return pl.BlockSpec((tm, d), lambda i, c: (i, 0))

    def full(shape):
        return pl.BlockSpec(shape, lambda i, c: (0, 0))

    return pl.pallas_call(
        kern,
        grid=(t // tm, nh),
        in_specs=[tok(), tok(), tok(), tok(), full(w_out.shape), full(g.shape),
                  pl.BlockSpec((d, th), lambda i, c: (0, c)),
                  pl.BlockSpec((d, th), lambda i, c: (0, c + nh)),
                  pl.BlockSpec((th, d), lambda i, c: (c, 0)),
                  full(fg.shape)],
        out_specs=pl.BlockSpec((tm, d), lambda i, c: (i, 0)),
        out_shape=jax.ShapeDtypeStruct((t, d), F32),
        scratch_shapes=[pltpu.VMEM((tm, d), F32), pltpu.VMEM((tm, d), BF16), pltpu.VMEM((tm, d), F32)],
        compiler_params=_cparams(("parallel", "arbitrary")),
        name="mix_ffn",
    )(x2, yc, yg, yd, w_out, g, w_ffn_in, w_ffn_in, w_ffn_out, fg)


def _regroup_in_proj(w, d, hdk, rank):
    widths = (d, d, hdk, hdk, d, d, rank, d, d, d, d, d, d)
    names = ("c_val", "c_gate", "g_q", "g_k", "g_v", "g_og", "g_f", "d_q", "d_k", "d_v", "m_conv", "m_gla", "m_diff")
    offs = [0]
    for wd in widths:
        offs.append(offs[-1] + wd)
    part = {n: w[:, offs[i]:offs[i + 1]] for i, n in enumerate(names)}
    order = ("c_val", "c_gate", "m_conv", "g_v", "g_og", "m_gla", "d_v", "m_diff", "d_q", "d_k", "g_q", "g_k")
    w_main = jnp.concatenate([part[n] for n in order], axis=1).astype(BF16)
    w_f = jnp.pad(part["g_f"], ((0, 0), (0, LANES - rank))).astype(BF16)
    return w_main, w_f


def kernel(x, norm_mix_g, w_in, conv_dw, conv_ln_g, conv_ln_b, w_conv_out, gla_wf_up, gla_bf, gla_norm_g,
           diff_lq1, diff_lk1, diff_lq2, diff_lk2, diff_norm_g, w_out, norm_ffn_g, w_ffn_in, w_ffn_out,
           final_norm_g):
    bsz, seq, d = x.shape
    depth = w_in.shape[0]
    hdk = gla_wf_up.shape[2]
    rank = gla_wf_up.shape[1]
    t = bsz * seq

    blk = {"c_val": 0, "c_gate": 1, "m_conv": 2, "g_v": 3, "g_og": 4, "m_gla": 5,
           "g_q": 10 * d // hdk, "g_k": 10 * d // hdk + 1}
    vd = diff_norm_g.shape[1]
    per = d // vd
    blk.update({"d_v": 6 * per, "m_diff": 7 * per, "d_q": 8 * per, "d_k": 9 * per})

    x2 = x.reshape(t, d)
    for l in range(depth):
        lambda_init = 0.8 - 0.6 * math.exp(-0.3 * l)
        w_main, w_f = _regroup_in_proj(w_in[l], d, hdk, rank)
        proj, f_low = _inproj(x2, norm_mix_g[l][None, :], w_main, w_f, tm=min(1024, t), tn=1024)

        y_conv = _conv(proj, conv_dw[l], conv_ln_g[l][None, :], conv_ln_b[l][None, :],
                       w_conv_out[l].astype(BF16), bsz=bsz, seq=seq, ts=min(512, seq), rc=32, blk=blk)

        wf_up = jnp.pad(gla_wf_up[l], ((0, LANES - rank), (0, 0))).astype(BF16)
        y_gla = _gla(proj, f_low, wf_up, gla_bf[l][None, :], gla_norm_g[l][None, :],
                     bsz=bsz, seq=seq, tg=min(512, seq), blk=blk)

        lam = _diff_lambda(diff_lq1[l], diff_lk1[l], diff_lq2[l], diff_lk2[l], lambda_init)
        y_diff = _diff(lam, proj, diff_norm_g[l][None, :], bsz=bsz, seq=seq, tq=min(512, seq),
                       lambda_init=lambda_init, blk=blk)

        x2 = _mix_ffn(x2, y_conv, y_gla, y_diff, w_out[l].astype(BF16), norm_ffn_g[l][None, :],
                      w_ffn_in[l].astype(BF16), w_ffn_out[l].astype(BF16), final_norm_g[None, :],
                      tm=min(1024, t), th=256, final=(l == depth - 1))
    return x2.reshape(bsz, seq, d)
```

```python
import functools
import math

import jax
import jax.numpy as jnp
from jax import lax
from jax.experimental import pallas as pl
from jax.experimental.pallas import tpu as pltpu

F32 = jnp.float32
BF16 = jnp.bfloat16

EPS = 1e-6
CHUNK = 64
CONV_KERNEL = 31
CONV_HALO = 32
GLA_HEADS = 4
GLA_RANK = 16
GLA_GATE_NORM = 16.0
DIFF_HEADS = 8
LANES = 128
SUBLANES = 8
VMEM_LIMIT = 56 * 1024 * 1024
LOG2E = 1.4426950408889634

TILES = {
    "inproj": dict(tm=1024, tn=1024),
    "conv": dict(ts=512, rc=32, rb=128),
    "gla": dict(tg=512),
    "diff": dict(tq=512, qs=256, hp=4),
    "mix_ffn": dict(tm=1024, th=256),
}


def _cparams(sem):
    return pltpu.CompilerParams(dimension_semantics=sem, vmem_limit_bytes=VMEM_LIMIT)


def _sigmoid(x):
    return 1.0 / (1.0 + jnp.exp(-x))


def _log_sigmoid(x):
    return jnp.minimum(x, 0.0) - jnp.log(1.0 + jnp.exp(-jnp.abs(x)))


def _split_bf16(x):
    hi = x.astype(BF16)
    lo = (x - hi.astype(F32)).astype(BF16)
    return hi, lo


def _inproj_kernel(x_ref, g_ref, w_ref, wf_ref, o_ref, f_ref, h_scr):
    @pl.when(pl.program_id(1) == 0)
    def _():
        x = x_ref[...]
        ms = jnp.mean(x * x, axis=-1, keepdims=True)
        h = (x * lax.rsqrt(ms + EPS) * g_ref[...]).astype(BF16)
        h_scr[...] = h
        f_ref[...] = jnp.dot(h, wf_ref[...], preferred_element_type=F32)

    o_ref[...] = jnp.dot(h_scr[...], w_ref[...], preferred_element_type=F32).astype(BF16)


def _inproj(x2, g, w_main, w_f, *, tm, tn):
    t, d = x2.shape
    tm = min(tm, t)
    nw = w_main.shape[1]
    return pl.pallas_call(
        _inproj_kernel,
        grid=(t // tm, nw // tn),
        in_specs=[
            pl.BlockSpec((tm, d), lambda i, j: (i, 0)),
            pl.BlockSpec((1, d), lambda i, j: (0, 0)),
            pl.BlockSpec((d, tn), lambda i, j: (0, j)),
            pl.BlockSpec((d, LANES), lambda i, j: (0, 0)),
        ],
        out_specs=[
            pl.BlockSpec((tm, tn), lambda i, j: (i, j)),
            pl.BlockSpec((tm, LANES), lambda i, j: (i, 0)),
        ],
        out_shape=[
            jax.ShapeDtypeStruct((t, nw), BF16),
            jax.ShapeDtypeStruct((t, LANES), F32),
        ],
        scratch_shapes=[pltpu.VMEM((tm, d), BF16)],
        compiler_params=_cparams(("parallel", "arbitrary")),
        name="inproj",
    )(x2, g, w_main, w_f)


def _conv_kernel(a_ref, gt_ref, m_ref, wdw_ref, lng_ref, lnb_ref, wpw_ref, o_ref, u_scr, c_scr, v_scr,
                 *, ts, rc, rb):
    d = a_ref.shape[1]

    @pl.when(pl.program_id(1) == 0)
    def _():
        u_scr[0:CONV_HALO, :] = jnp.zeros((CONV_HALO, d), F32)

    def glu_body(r, carry):
        r0 = pl.multiple_of(r * rc, rc)
        a = a_ref[pl.ds(r0, rc), :].astype(F32)
        gt = gt_ref[pl.ds(r0, rc), :].astype(F32)
        u_scr[pl.ds(CONV_HALO + r0, rc), :] = a * _sigmoid(gt)
        return carry

    lax.fori_loop(0, ts // rc, glu_body, 0)

    first = CONV_HALO - (CONV_KERNEL - 1)

    def conv_body(i, carry):
        t0 = pl.multiple_of(i * rb, rb)
        for s in range(d // LANES):
            lanes = slice(s * LANES, (s + 1) * LANES)
            acc = None
            for r in range(SUBLANES):
                rows = rb + (SUBLANES if r else 0)
                part = None
                for j in range(first, first + CONV_KERNEL):
                    if j % SUBLANES != r:
                        continue
                    term = (u_scr[pl.ds(t0 + j - r, rows), lanes] * wdw_ref[j - first:j - first + 1, lanes])
                    part = term if part is None else part + term
                if part is None:
                    continue
                if r:
                    part = pltpu.roll(part, rows - r, 0)[0:rb]
                acc = part if acc is None else acc + part
            c_scr[pl.ds(t0, rb), lanes] = acc
        return carry

    lax.fori_loop(0, ts // rb, conv_body, 0)
    u_scr[0:CONV_HALO, :] = u_scr[ts:ts + CONV_HALO, :]

    lng = lng_ref[...]
    lnb = lnb_ref[...]

    def ln_body(r, carry):
        r0 = pl.multiple_of(r * rc, rc)
        x = c_scr[pl.ds(r0, rc), :]
        mu = jnp.mean(x, axis=-1, keepdims=True)
        xc = x - mu
        var = jnp.mean(xc * xc, axis=-1, keepdims=True)
        y = xc * lax.rsqrt(var + EPS) * lng + lnb
        v_scr[pl.ds(r0, rc), :] = (y * _sigmoid(y)).astype(BF16)
        return carry

    lax.fori_loop(0, ts // rc, ln_body, 0, unroll=4)

    o = jnp.dot(v_scr[...], wpw_ref[...], preferred_element_type=F32)
    o_ref[...] = (o * _sigmoid(m_ref[...].astype(F32))).astype(BF16)


def _conv(proj, wdw, lng, lnb, wpw, *, bsz, seq, ts, rc, rb, blk):
    d = wpw.shape[0]
    ts = min(ts, seq)
    nst = seq // ts
    kern = functools.partial(_conv_kernel, ts=ts, rc=rc, rb=min(rb, ts))

    def col(c):
        return pl.BlockSpec((ts, d), lambda b, s: (b * nst + s, c))

    def full(shape):
        return pl.BlockSpec(shape, lambda b, s: (0, 0))

    return pl.pallas_call(
        kern,
        grid=(bsz, nst),
        in_specs=[col(blk["c_val"]), col(blk["c_gate"]), col(blk["m_conv"]),
                  full(wdw.shape), full(lng.shape), full(lnb.shape), full(wpw.shape)],
        out_specs=pl.BlockSpec((ts, d), lambda b, s: (b * nst + s, 0)),
        out_shape=jax.ShapeDtypeStruct((bsz * seq, d), BF16),
        scratch_shapes=[pltpu.VMEM((CONV_HALO + ts, d), F32), pltpu.VMEM((ts, d), F32), pltpu.VMEM((ts, d), BF16)],
        compiler_params=_cparams(("parallel", "arbitrary")),
        name="conv",
    )(proj, proj, proj, wdw, lng, lnb, wpw)


def _gla_kernel(q_ref, k_ref, v_ref, og_ref, m_ref, f_ref, wf_ref, bf_ref, ng_ref, o_ref, st_scr, *, tg, dk, dv):
    @pl.when(pl.program_id(1) == 0)
    def _():
        st_scr[...] = jnp.zeros(st_scr.shape, F32)

    row = lax.broadcasted_iota(jnp.int32, (CHUNK, CHUNK), 0)
    col = lax.broadcasted_iota(jnp.int32, (CHUNK, CHUNK), 1)
    causal = row >= col
    tril = jnp.where(causal, 1.0, 0.0).astype(BF16)
    wf = wf_ref[...]
    bf = bf_ref[...]
    ng = ng_ref[...]
    qscale = dk ** -0.5

    def chunk(c, carry):
        r0 = pl.multiple_of(c * CHUNK, CHUNK)
        rows = pl.ds(r0, CHUNK)
        f_hi, f_lo = _split_bf16(f_ref[rows, :])
        logit = (jnp.dot(f_hi, wf, preferred_element_type=F32)
                 + jnp.dot(f_lo, wf, preferred_element_type=F32) + bf)
        logf = _log_sigmoid(logit) * (1.0 / GLA_GATE_NORM)
        l_hi, l_lo = _split_bf16(logf)
        g = (jnp.dot(tril, l_hi, preferred_element_type=F32)
             + jnp.dot(tril, l_lo, preferred_element_type=F32))
        g_last = g[CHUNK - 1:CHUNK, :]
        q = q_ref[rows, :].astype(F32) * qscale
        k = k_ref[rows, :].astype(F32)
        q_g = (q * jnp.exp(g)).astype(BF16)
        k_g = (k * jnp.exp(-g)).astype(BF16)
        k_d = (k * jnp.exp(g_last - g)).astype(BF16)
        decay = jnp.exp(g_last)
        for h in range(GLA_HEADS):
            ks = slice(h * dk, (h + 1) * dk)
            vs = slice(h * dv, (h + 1) * dv)
            v = v_ref[rows, vs]
            att = lax.dot_general(q_g[:, ks], k_g[:, ks], (((1,), (1,)), ((), ())),
                                  preferred_element_type=F32)
            att = jnp.where(causal, att, 0.0).astype(BF16)
            st = st_scr[h]
            o = (jnp.dot(att, v, preferred_element_type=F32)
                 + lax.dot_general(q_g[:, ks], st.astype(BF16), (((1,), (1,)), ((), ())),
                                   preferred_element_type=F32))
            st_scr[h] = st * decay[:, ks] + lax.dot_general(
                v, k_d[:, ks], (((0,), (0,)), ((), ())), preferred_element_type=F32)
            o = o * lax.rsqrt(jnp.mean(o * o, axis=-1, keepdims=True) + EPS) * ng
            og = og_ref[rows, vs].astype(F32)
            gate = _sigmoid(m_ref[rows, vs].astype(F32))
            o_ref[rows, vs] = (gate * (o * (og * _sigmoid(og)))).astype(BF16)
        return carry

    lax.fori_loop(0, tg // CHUNK, chunk, 0, unroll=2)


def _gla(proj, f_low, wf_up, bf, ng, *, bsz, seq, tg, blk):
    hdk = wf_up.shape[1]
    dk = hdk // GLA_HEADS
    dv = ng.shape[1]
    hdv = GLA_HEADS * dv
    tg = min(tg, seq)
    nst = seq // tg
    kern = functools.partial(_gla_kernel, tg=tg, dk=dk, dv=dv)

    def col(width, c):
        return pl.BlockSpec((tg, width), lambda b, s: (b * nst + s, c))

    def full(shape):
        return pl.BlockSpec(shape, lambda b, s: (0, 0))

    return pl.pallas_call(
        kern,
        grid=(bsz, nst),
        in_specs=[col(hdk, blk["g_q"]), col(hdk, blk["g_k"]), col(hdv, blk["g_v"]), col(hdv, blk["g_og"]),
                  col(hdv, blk["m_gla"]), col(LANES, 0),
                  full(wf_up.shape), full(bf.shape), full(ng.shape)],
        out_specs=pl.BlockSpec((tg, hdv), lambda b, s: (b * nst + s, 0)),
        out_shape=jax.ShapeDtypeStruct((bsz * seq, hdv), BF16),
        scratch_shapes=[pltpu.VMEM((GLA_HEADS, dv, dk), F32)],
        compiler_params=_cparams(("parallel", "arbitrary")),
        name="gla",
    )(proj, proj, proj, proj, proj, f_low, wf_up, bf, ng)


def _diff_kernel(lam_ref, q_ref, k_ref, v_ref, m_ref, ng_ref, o_ref, vt_scr, qq_scr, m_scr, l_scr, acc_scr,
                 *, tq, qs, hp, hd, lambda_init):
    qi = pl.program_id(2)
    seq = k_ref.shape[0]
    vd = 2 * hd
    heads = [slice(h * vd, (h + 1) * vd) for h in range(hp)]
    strips = [(h, c) for h in range(hp) for c in range(2 * tq // qs)]

    @pl.when(qi == 0)
    def _():
        for h in range(hp):
            for c in range(seq // tq):
                rows = slice(c * tq, (c + 1) * tq)
                vt_scr[h, :, rows] = v_ref[rows, heads[h]].astype(F32).T.astype(BF16)

    for h in range(hp):
        q = q_ref[:, heads[h]].astype(F32) * (hd ** -0.5 * LOG2E)
        lane = lax.broadcasted_iota(jnp.int32, q.shape, 1)
        qq_scr[h, 0:tq, :] = jnp.where(lane < hd, q, 0.0).astype(BF16)
        qq_scr[h, tq:2 * tq, :] = jnp.where(lane >= hd, q, 0.0).astype(BF16)

    m_scr[...] = jnp.full(m_scr.shape, -jnp.inf, F32)
    l_scr[...] = jnp.zeros(l_scr.shape, F32)
    acc_scr[...] = jnp.zeros(acc_scr.shape, F32)

    def step(j, masked):
        c0 = pl.multiple_of(j * tq, tq)

        def keys(c):
            q0 = (c * qs) % tq
            return q0, (min(tq, q0 + qs) if masked else tq)

        sts = []
        for h, c in strips:
            _, nk = keys(c)
            sts.append(lax.dot_general(k_ref[pl.ds(c0, nk), heads[h]], qq_scr[h, c * qs:(c + 1) * qs, :],
                                       (((1,), (1,)), ((), ())), preferred_element_type=F32))
        for (h, c), st in zip(strips, sts):
            cols = slice(c * qs, (c + 1) * qs)
            q0, nk = keys(c)
            if masked:
                kc = lax.broadcasted_iota(jnp.int32, st.shape, 0) // CHUNK
                qc = (lax.broadcasted_iota(jnp.int32, st.shape, 1) + q0) // CHUNK
                st = jnp.where(kc <= qc, st, -jnp.inf)
            m_prev = m_scr[h, :, cols]
            m_new = jnp.maximum(m_prev, jnp.max(st, axis=0, keepdims=True))
            alpha = jnp.exp2(m_prev - m_new)
            pt = jnp.exp2(st - m_new)
            l_scr[h, :, cols] = alpha * l_scr[h, :, cols] + jnp.sum(pt, axis=0, keepdims=True)
            acc_scr[h, :, cols] = alpha * acc_scr[h, :, cols] + jnp.dot(
                vt_scr[h, :, pl.ds(c0, nk)], pt.astype(BF16), preferred_element_type=F32)
            m_scr[h, :, cols] = m_new

    def body(j, carry):
        step(j, False)
        return carry

    lax.fori_loop(0, qi, body, 0)
    step(qi, True)

    lam = lam_ref[0]
    for h in range(hp):
        ot = acc_scr[h] / l_scr[h]
        od = (ot[:, 0:tq] - lam * ot[:, tq:2 * tq]).T
        od = od * lax.rsqrt(jnp.mean(od * od, axis=-1, keepdims=True) + EPS) * ng_ref[...] * (1.0 - lambda_init)
        o_ref[:, heads[h]] = (od * _sigmoid(m_ref[:, heads[h]].astype(F32))).astype(BF16)


def _diff(lam, proj, ng, *, bsz, seq, tq, qs, hp, lambda_init, blk):
    vd = ng.shape[1]
    hd = vd // 2
    tq = min(tq, seq)
    nq = seq // tq
    w = hp * vd
    kern = functools.partial(_diff_kernel, tq=tq, qs=qs, hp=hp, hd=hd, lambda_init=lambda_init)
    return pl.pallas_call(
        kern,
        grid=(bsz, DIFF_HEADS // hp, nq),
        in_specs=[
            pl.BlockSpec(memory_space=pltpu.SMEM),
            pl.BlockSpec((tq, w), lambda b, h, i: (b * nq + i, blk["d_q"] // hp + h)),
            pl.BlockSpec((seq, w), lambda b, h, i: (b, blk["d_k"] // hp + h)),
            pl.BlockSpec((seq, w), lambda b, h, i: (b, blk["d_v"] // hp + h)),
            pl.BlockSpec((tq, w), lambda b, h, i: (b * nq + i, blk["m_diff"] // hp + h)),
            pl.BlockSpec((1, vd), lambda b, h, i: (0, 0)),
        ],
        out_specs=pl.BlockSpec((tq, w), lambda b, h, i: (b * nq + i, h)),
        out_shape=jax.ShapeDtypeStruct((bsz * seq, DIFF_HEADS * vd), BF16),
        scratch_shapes=[pltpu.VMEM((hp, vd, seq), BF16), pltpu.VMEM((hp, 2 * tq, vd), BF16),
                        pltpu.VMEM((hp, 1, 2 * tq), F32), pltpu.VMEM((hp, 1, 2 * tq), F32),
                        pltpu.VMEM((hp, vd, 2 * tq), F32)],
        compiler_params=_cparams(("parallel", "parallel", "arbitrary")),
        name="diff",
    )(lam, proj, proj, proj, proj, ng)


def _lambda_kernel(p_ref, o_ref, *, lambda_init):
    p = p_ref[...]
    s1 = jnp.sum(p[0:1, :] * p[1:2, :], axis=-1, keepdims=True)
    s2 = jnp.sum(p[2:3, :] * p[3:4, :], axis=-1, keepdims=True)
    o_ref[...] = jnp.exp(s1) - jnp.exp(s2) + lambda_init


def _diff_lambda(lq1, lk1, lq2, lk2, lambda_init):
    p = jnp.stack([lq1, lk1, lq2, lk2]).astype(F32)
    out = pl.pallas_call(
        functools.partial(_lambda_kernel, lambda_init=lambda_init),
        out_shape=jax.ShapeDtypeStruct((1, 1), F32),
        name="diff_lambda",
    )(p)
    return out.reshape(1)


def _mix_ffn_kernel(x_ref, yc_ref, yg_ref, yd_ref, wo_ref, g_ref, wg_ref, wu_ref, wd_ref, fg_ref, o_ref,
                    x1_scr, h_scr, acc_scr, *, final):
    c = pl.program_id(1)

    @pl.when(c == 0)
    def _():
        y = (yc_ref[...].astype(F32) + yg_ref[...].astype(F32) + yd_ref[...].astype(F32)).astype(BF16)
        x1 = x_ref[...] + jnp.dot(y, wo_ref[...], preferred_element_type=F32)
        x1_scr[...] = x1
        ms = jnp.mean(x1 * x1, axis=-1, keepdims=True)
        h_scr[...] = (x1 * lax.rsqrt(ms + EPS) * g_ref[...]).astype(BF16)
        acc_scr[...] = jnp.zeros(acc_scr.shape, F32)

    h = h_scr[...]
    gate = jnp.dot(h, wg_ref[...], preferred_element_type=F32)
    up = jnp.dot(h, wu_ref[...], preferred_element_type=F32)
    act = (gate * _sigmoid(gate) * up).astype(BF16)
    acc_scr[...] += jnp.dot(act, wd_ref[...], preferred_element_type=F32)

    @pl.when(c == pl.num_programs(1) - 1)
    def _():
        x2 = x1_scr[...] + acc_scr[...]
        if final:
            ms = jnp.mean(x2 * x2, axis=-1, keepdims=True)
            x2 = x2 * lax.rsqrt(ms + EPS) * fg_ref[...]
        o_ref[...] = x2


def _mix_ffn(x2, yc, yg, yd, w_out, g, w_ffn_in, w_ffn_out, fg, *, tm, th, final):
    t, d = x2.shape
    tm = min(tm, t)
    hidden = w_ffn_out.shape[0]
    nh = hidden // th
    kern = functools.partial(_mix_ffn_kernel, final=final)

    def tok():
        return pl.BlockSpec((tm, d), lambda i, c: (i, 0))

    def full(shape):
        return pl.BlockSpec(shape, lambda i, c: (0, 0))

    return pl.pallas_call(
        kern,
        grid=(t // tm, nh),
        in_specs=[tok(), tok(), tok(), tok(), full(w_out.shape), full(g.shape),
                  pl.BlockSpec((d, th), lambda i, c: (0, c)),
                  pl.BlockSpec((d, th), lambda i, c: (0, c + nh)),
                  pl.BlockSpec((th, d), lambda i, c: (c, 0)),
                  full(fg.shape)],
        out_specs=pl.BlockSpec((tm, d), lambda i, c: (i, 0)),
        out_shape=jax.ShapeDtypeStruct((t, d), F32),
        scratch_shapes=[pltpu.VMEM((tm, d), F32), pltpu.VMEM((tm, d), BF16), pltpu.VMEM((tm, d), F32)],
        compiler_params=_cparams(("parallel", "arbitrary")),
        name="mix_ffn",
    )(x2, yc, yg, yd, w_out, g, w_ffn_in, w_ffn_in, w_ffn_out, fg)


def _regroup_in_proj(w, d, hdk, rank):
    widths = (d, d, hdk, hdk, d, d, rank, d, d, d, d, d, d)
    names = ("c_val", "c_gate", "g_q", "g_k", "g_v", "g_og", "g_f", "d_q", "d_k", "d_v", "m_conv", "m_gla", "m_diff")
    offs = [0]
    for wd in widths:
        offs.append(offs[-1] + wd)
    part = {n: w[:, offs[i]:offs[i + 1]] for i, n in enumerate(names)}
    order = ("c_val", "c_gate", "m_conv", "g_v", "g_og", "m_gla", "d_v", "m_diff", "d_q", "d_k", "g_q", "g_k")
    w_main = jnp.concatenate([part[n] for n in order], axis=1).astype(BF16)
    w_f = jnp.pad(part["g_f"], ((0, 0), (0, LANES - rank))).astype(BF16)
    return w_main, w_f


def kernel(x, norm_mix_g, w_in, conv_dw, conv_ln_g, conv_ln_b, w_conv_out, gla_wf_up, gla_bf, gla_norm_g,
           diff_lq1, diff_lk1, diff_lq2, diff_lk2, diff_norm_g, w_out, norm_ffn_g, w_ffn_in, w_ffn_out,
           final_norm_g):
    bsz, seq, d = x.shape
    depth = w_in.shape[0]
    hdk = gla_wf_up.shape[2]
    rank = gla_wf_up.shape[1]
    t = bsz * seq

    blk = {"c_val": 0, "c_gate": 1, "m_conv": 2, "g_v": 3, "g_og": 4, "m_gla": 5,
           "g_q": 10 * d // hdk, "g_k": 10 * d // hdk + 1}
    vd = diff_norm_g.shape[1]
    per = d // vd
    blk.update({"d_v": 6 * per, "m_diff": 7 * per, "d_q": 8 * per, "d_k": 9 * per})

    x2 = x.reshape(t, d)
    for l in range(depth):
        lambda_init = 0.8 - 0.6 * math.exp(-0.3 * l)
        w_main, w_f = _regroup_in_proj(w_in[l], d, hdk, rank)
        proj, f_low = _inproj(x2, norm_mix_g[l][None, :], w_main, w_f, **TILES["inproj"])

        y_conv = _conv(proj, conv_dw[l], conv_ln_g[l][None, :], conv_ln_b[l][None, :],
                       w_conv_out[l].astype(BF16), bsz=bsz, seq=seq, blk=blk, **TILES["conv"])

        wf_up = jnp.pad(gla_wf_up[l], ((0, LANES - rank), (0, 0))).astype(BF16)
        y_gla = _gla(proj, f_low, wf_up, gla_bf[l][None, :], gla_norm_g[l][None, :],
                     bsz=bsz, seq=seq, blk=blk, **TILES["gla"])

        lam = _diff_lambda(diff_lq1[l], diff_lk1[l], diff_lq2[l], diff_lk2[l], lambda_init)
        y_diff = _diff(lam, proj, diff_norm_g[l][None, :], bsz=bsz, seq=seq,
                       lambda_init=lambda_init, blk=blk, **TILES["diff"])

        x2 = _mix_ffn(x2, y_conv, y_gla, y_diff, w_out[l].astype(BF16), norm_ffn_g[l][None, :],
                      w_ffn_in[l].astype(BF16), w_ffn_out[l].astype(BF16), final_norm_g[None, :],
                      final=(l == depth - 1), **TILES["mix_ffn"])
    return x2.reshape(bsz, seq, d)
```

```python
import functools
import math

import jax
import jax.numpy as jnp
from jax import lax
from jax.experimental import pallas as pl
from jax.experimental.pallas import tpu as pltpu

F32 = jnp.float32
BF16 = jnp.bfloat16

EPS = 1e-6
CHUNK = 64
CONV_KERNEL = 31
CONV_HALO = 32
GLA_HEADS = 4
GLA_RANK = 16
GLA_GATE_NORM = 16.0
DIFF_HEADS = 8
LANES = 128
SUBLANES = 8
VMEM_LIMIT = 56 * 1024 * 1024
LOG2E = 1.4426950408889634

TILES = {
    "inproj": dict(tm=2048, tn=1024),
    "conv": dict(ts=512, rc=32, rb=128),
    "gla": dict(tg=512),
    "diff": dict(tq=512, qs=256, hp=4),
    "mix_ffn": dict(tm=512, th=256),
}


def _cparams(sem):
    return pltpu.CompilerParams(dimension_semantics=sem, vmem_limit_bytes=VMEM_LIMIT)


def _sigmoid(x):
    return 1.0 / (1.0 + jnp.exp(-x))


def _log_sigmoid(x):
    return jnp.minimum(x, 0.0) - jnp.log(1.0 + jnp.exp(-jnp.abs(x)))


def _split_bf16(x):
    hi = x.astype(BF16)
    lo = (x - hi.astype(F32)).astype(BF16)
    return hi, lo


def _inproj_kernel(x_ref, g_ref, w_ref, wf_ref, o_ref, f_ref, h_scr):
    @pl.when(pl.program_id(1) == 0)
    def _():
        x = x_ref[...]
        ms = jnp.mean(x * x, axis=-1, keepdims=True)
        h = (x * lax.rsqrt(ms + EPS) * g_ref[...]).astype(BF16)
        h_scr[...] = h
        f_ref[...] = jnp.dot(h, wf_ref[...], preferred_element_type=F32)

    o_ref[...] = jnp.dot(h_scr[...], w_ref[...], preferred_element_type=F32).astype(BF16)


def _inproj(x2, g, w_main, w_f, *, tm, tn):
    t, d = x2.shape
    tm = min(tm, t)
    nw = w_main.shape[1]
    return pl.pallas_call(
        _inproj_kernel,
        grid=(t // tm, nw // tn),
        in_specs=[
            pl.BlockSpec((tm, d), lambda i, j: (i, 0)),
            pl.BlockSpec((1, d), lambda i, j: (0, 0)),
            pl.BlockSpec((d, tn), lambda i, j: (0, j)),
            pl.BlockSpec((d, LANES), lambda i, j: (0, 0)),
        ],
        out_specs=[
            pl.BlockSpec((tm, tn), lambda i, j: (i, j)),
            pl.BlockSpec((tm, LANES), lambda i, j: (i, 0)),
        ],
        out_shape=[
            jax.ShapeDtypeStruct((t, nw), BF16),
            jax.ShapeDtypeStruct((t, LANES), F32),
        ],
        scratch_shapes=[pltpu.VMEM((tm, d), BF16)],
        compiler_params=_cparams(("parallel", "arbitrary")),
        name="inproj",
    )(x2, g, w_main, w_f)


def _conv_kernel(a_ref, gt_ref, m_ref, wdw_ref, lng_ref, lnb_ref, wpw_ref, o_ref, u_scr, c_scr, v_scr,
                 *, ts, rc, rb):
    d = a_ref.shape[1]

    @pl.when(pl.program_id(1) == 0)
    def _():
        u_scr[0:CONV_HALO, :] = jnp.zeros((CONV_HALO, d), F32)

    def glu_body(r, carry):
        r0 = pl.multiple_of(r * rc, rc)
        a = a_ref[pl.ds(r0, rc), :].astype(F32)
        gt = gt_ref[pl.ds(r0, rc), :].astype(F32)
        u_scr[pl.ds(CONV_HALO + r0, rc), :] = a * _sigmoid(gt)
        return carry

    lax.fori_loop(0, ts // rc, glu_body, 0)

    first = CONV_HALO - (CONV_KERNEL - 1)

    def conv_body(i, carry):
        t0 = pl.multiple_of(i * rb, rb)
        for s in range(d // LANES):
            lanes = slice(s * LANES, (s + 1) * LANES)
            acc = None
            for r in range(SUBLANES):
                rows = rb + (SUBLANES if r else 0)
                part = None
                for j in range(first, first + CONV_KERNEL):
                    if j % SUBLANES != r:
                        continue
                    term = (u_scr[pl.ds(t0 + j - r, rows), lanes] * wdw_ref[j - first:j - first + 1, lanes])
                    part = term if part is None else part + term
                if part is None:
                    continue
                if r:
                    part = pltpu.roll(part, rows - r, 0)[0:rb]
                acc = part if acc is None else acc + part
            c_scr[pl.ds(t0, rb), lanes] = acc
        return carry

    lax.fori_loop(0, ts // rb, conv_body, 0)
    u_scr[0:CONV_HALO, :] = u_scr[ts:ts + CONV_HALO, :]

    lng = lng_ref[...]
    lnb = lnb_ref[...]

    def ln_body(r, carry):
        r0 = pl.multiple_of(r * rc, rc)
        x = c_scr[pl.ds(r0, rc), :]
        mu = jnp.mean(x, axis=-1, keepdims=True)
        xc = x - mu
        var = jnp.mean(xc * xc, axis=-1, keepdims=True)
        y = xc * lax.rsqrt(var + EPS) * lng + lnb
        v_scr[pl.ds(r0, rc), :] = (y * _sigmoid(y)).astype(BF16)
        return carry

    lax.fori_loop(0, ts // rc, ln_body, 0, unroll=4)

    o = jnp.dot(v_scr[...], wpw_ref[...], preferred_element_type=F32)
    o_ref[...] = (o * _sigmoid(m_ref[...].astype(F32))).astype(BF16)


def _conv(proj, wdw, lng, lnb, wpw, *, bsz, seq, ts, rc, rb, blk):
    d = wpw.shape[0]
    ts = min(ts, seq)
    nst = seq // ts
    kern = functools.partial(_conv_kernel, ts=ts, rc=rc, rb=min(rb, ts))

    def col(c):
        return pl.BlockSpec((ts, d), lambda b, s: (b * nst + s, c))

    def full(shape):
        return pl.BlockSpec(shape, lambda b, s: (0, 0))

    return pl.pallas_call(
        kern,
        grid=(bsz, nst),
        in_specs=[col(blk["c_val"]), col(blk["c_gate"]), col(blk["m_conv"]),
                  full(wdw.shape), full(lng.shape), full(lnb.shape), full(wpw.shape)],
        out_specs=pl.BlockSpec((ts, d), lambda b, s: (b * nst + s, 0)),
        out_shape=jax.ShapeDtypeStruct((bsz * seq, d), BF16),
        scratch_shapes=[pltpu.VMEM((CONV_HALO + ts, d), F32), pltpu.VMEM((ts, d), F32), pltpu.VMEM((ts, d), BF16)],
        compiler_params=_cparams(("parallel", "arbitrary")),
        name="conv",
    )(proj, proj, proj, wdw, lng, lnb, wpw)


def _gla_kernel(q_ref, k_ref, v_ref, og_ref, m_ref, f_ref, wf_ref, bf_ref, ng_ref, o_ref,
                st_scr, qg_scr, kg_scr, kd_scr, *, tg, dk, dv):
    nc = tg // CHUNK

    @pl.when(pl.program_id(1) == 0)
    def _():
        st_scr[...] = jnp.zeros(st_scr.shape, F32)

    row = lax.broadcasted_iota(jnp.int32, (tg, tg), 0)
    col = lax.broadcasted_iota(jnp.int32, (tg, tg), 1)
    same = jnp.where(row // CHUNK == col // CHUNK, 1.0, 0.0)
    causal = jnp.where(row >= col, same, 0.0)
    chunk_cumsum = causal.astype(BF16)

    f_hi, f_lo = _split_bf16(f_ref[...])
    wf = wf_ref[...]
    logit = (jnp.dot(f_hi, wf, preferred_element_type=F32)
             + jnp.dot(f_lo, wf, preferred_element_type=F32) + bf_ref[...])
    logf = _log_sigmoid(logit) * (1.0 / GLA_GATE_NORM)
    l_hi, l_lo = _split_bf16(logf)
    g = (jnp.dot(chunk_cumsum, l_hi, preferred_element_type=F32)
         + jnp.dot(chunk_cumsum, l_lo, preferred_element_type=F32))
    last = [g[(c + 1) * CHUNK - 1:(c + 1) * CHUNK, :] for c in range(nc)]
    g_last = jnp.concatenate([jnp.broadcast_to(r, (CHUNK, r.shape[1])) for r in last], axis=0)
    q = q_ref[...].astype(F32) * (dk ** -0.5)
    k = k_ref[...].astype(F32)
    qg_scr[...] = (q * jnp.exp(g)).astype(BF16)
    kg_scr[...] = (k * jnp.exp(-g)).astype(BF16)
    kd_scr[...] = (k * jnp.exp(g_last - g)).astype(BF16)
    decay = [jnp.exp(r) for r in last]
    ng = ng_ref[...]

    for h in range(GLA_HEADS):
        ks = slice(h * dk, (h + 1) * dk)
        vs = slice(h * dv, (h + 1) * dv)
        att = lax.dot_general(qg_scr[:, ks], kg_scr[:, ks], (((1,), (1,)), ((), ())),
                              preferred_element_type=F32)
        att = jnp.where(causal > 0.0, att, 0.0).astype(BF16)
        o = jnp.dot(att, v_ref[:, vs], preferred_element_type=F32)
        st = st_scr[h]
        inter = []
        for c in range(nc):
            rows = slice(c * CHUNK, (c + 1) * CHUNK)
            inter.append(lax.dot_general(qg_scr[rows, ks], st.astype(BF16), (((1,), (1,)), ((), ())),
                                         preferred_element_type=F32))
            st = st * decay[c][:, ks] + lax.dot_general(
                v_ref[rows, vs], kd_scr[rows, ks], (((0,), (0,)), ((), ())), preferred_element_type=F32)
        st_scr[h] = st
        o = o + jnp.concatenate(inter, axis=0)
        o = o * lax.rsqrt(jnp.mean(o * o, axis=-1, keepdims=True) + EPS) * ng
        og = og_ref[:, vs].astype(F32)
        gate = _sigmoid(m_ref[:, vs].astype(F32))
        o_ref[:, vs] = (gate * (o * (og * _sigmoid(og)))).astype(BF16)


def _gla(proj, f_low, wf_up, bf, ng, *, bsz, seq, tg, blk):
    hdk = wf_up.shape[1]
    dk = hdk // GLA_HEADS
    dv = ng.shape[1]
    hdv = GLA_HEADS * dv
    tg = min(tg, seq)
    nst = seq // tg
    kern = functools.partial(_gla_kernel, tg=tg, dk=dk, dv=dv)

    def col(width, c):
        return pl.BlockSpec((tg, width), lambda b, s: (b * nst + s, c))

    def full(shape):
        return pl.BlockSpec(shape, lambda b, s: (0, 0))

    return pl.pallas_call(
        kern,
        grid=(bsz, nst),
        in_specs=[col(hdk, blk["g_q"]), col(hdk, blk["g_k"]), col(hdv, blk["g_v"]), col(hdv, blk["g_og"]),
                  col(hdv, blk["m_gla"]), col(LANES, 0),
                  full(wf_up.shape), full(bf.shape), full(ng.shape)],
        out_specs=pl.BlockSpec((tg, hdv), lambda b, s: (b * nst + s, 0)),
        out_shape=jax.ShapeDtypeStruct((bsz * seq, hdv), BF16),
        scratch_shapes=[pltpu.VMEM((GLA_HEADS, dv, dk), F32)] + [pltpu.VMEM((tg, hdk), BF16)] * 3,
        compiler_params=_cparams(("parallel", "arbitrary")),
        name="gla",
    )(proj, proj, proj, proj, proj, f_low, wf_up, bf, ng)


def _diff_kernel(lam_ref, q_ref, k_ref, v_ref, m_ref, ng_ref, o_ref, vt_scr, qq_scr, m_scr, l_scr, acc_scr,
                 *, tq, qs, hp, hd, lambda_init):
    qi = pl.program_id(2)
    seq = k_ref.shape[0]
    vd = 2 * hd
    heads = [slice(h * vd, (h + 1) * vd) for h in range(hp)]
    strips = [(h, c) for h in range(hp) for c in range(2 * tq // qs)]

    @pl.when(qi == 0)
    def _():
        for h in range(hp):
            for c in range(seq // tq):
                rows = slice(c * tq, (c + 1) * tq)
                vt_scr[h, :, rows] = v_ref[rows, heads[h]].astype(F32).T.astype(BF16)

    for h in range(hp):
        q = q_ref[:, heads[h]].astype(F32) * (hd ** -0.5 * LOG2E)
        lane = lax.broadcasted_iota(jnp.int32, q.shape, 1)
        qq_scr[h, 0:tq, :] = jnp.where(lane < hd, q, 0.0).astype(BF16)
        qq_scr[h, tq:2 * tq, :] = jnp.where(lane >= hd, q, 0.0).astype(BF16)

    m_scr[...] = jnp.full(m_scr.shape, -jnp.inf, F32)
    l_scr[...] = jnp.zeros(l_scr.shape, F32)
    acc_scr[...] = jnp.zeros(acc_scr.shape, F32)

    def step(j, masked):
        c0 = pl.multiple_of(j * tq, tq)

        def keys(c):
            q0 = (c * qs) % tq
            return q0, (min(tq, q0 + qs) if masked else tq)

        sts = []
        for h, c in strips:
            _, nk = keys(c)
            sts.append(lax.dot_general(k_ref[pl.ds(c0, nk), heads[h]], qq_scr[h, c * qs:(c + 1) * qs, :],
                                       (((1,), (1,)), ((), ())), preferred_element_type=F32))
        for (h, c), st in zip(strips, sts):
            cols = slice(c * qs, (c + 1) * qs)
            q0, nk = keys(c)
            if masked:
                kc = lax.broadcasted_iota(jnp.int32, st.shape, 0) // CHUNK
                qc = (lax.broadcasted_iota(jnp.int32, st.shape, 1) + q0) // CHUNK
                st = jnp.where(kc <= qc, st, -jnp.inf)
            m_prev = m_scr[h, :, cols]
            m_new = jnp.maximum(m_prev, jnp.max(st, axis=0, keepdims=True))
            alpha = jnp.exp2(m_prev - m_new)
            pt = jnp.exp2(st - m_new)
            l_scr[h, :, cols] = alpha * l_scr[h, :, cols] + jnp.sum(pt, axis=0, keepdims=True)
            acc_scr[h, :, cols] = alpha * acc_scr[h, :, cols] + jnp.dot(
                vt_scr[h, :, pl.ds(c0, nk)], pt.astype(BF16), preferred_element_type=F32)
            m_scr[h, :, cols] = m_new

    def body(j, carry):
        step(j, False)
        return carry

    lax.fori_loop(0, qi, body, 0)
    step(qi, True)

    lam = lam_ref[0]
    for h in range(hp):
        ot = acc_scr[h] / l_scr[h]
        od = (ot[:, 0:tq] - lam * ot[:, tq:2 * tq]).T
        od = od * lax.rsqrt(jnp.mean(od * od, axis=-1, keepdims=True) + EPS) * ng_ref[...] * (1.0 - lambda_init)
        o_ref[:, heads[h]] = (od * _sigmoid(m_ref[:, heads[h]].astype(F32))).astype(BF16)


def _diff(lam, proj, ng, *, bsz, seq, tq, qs, hp, lambda_init, blk):
    vd = ng.shape[1]
    hd = vd // 2
    tq = min(tq, seq)
    nq = seq // tq
    w = hp * vd
    kern = functools.partial(_diff_kernel, tq=tq, qs=qs, hp=hp, hd=hd, lambda_init=lambda_init)
    return pl.pallas_call(
        kern,
        grid=(bsz, DIFF_HEADS // hp, nq),
        in_specs=[
            pl.BlockSpec(memory_space=pltpu.SMEM),
            pl.BlockSpec((tq, w), lambda b, h, i: (b * nq + i, blk["d_q"] // hp + h)),
            pl.BlockSpec((seq, w), lambda b, h, i: (b, blk["d_k"] // hp + h)),
            pl.BlockSpec((seq, w), lambda b, h, i: (b, blk["d_v"] // hp + h)),
            pl.BlockSpec((tq, w), lambda b, h, i: (b * nq + i, blk["m_diff"] // hp + h)),
            pl.BlockSpec((1, vd), lambda b, h, i: (0, 0)),
        ],
        out_specs=pl.BlockSpec((tq, w), lambda b, h, i: (b * nq + i, h)),
        out_shape=jax.ShapeDtypeStruct((bsz * seq, DIFF_HEADS * vd), BF16),
        scratch_shapes=[pltpu.VMEM((hp, vd, seq), BF16), pltpu.VMEM((hp, 2 * tq, vd), BF16),
                        pltpu.VMEM((hp, 1, 2 * tq), F32), pltpu.VMEM((hp, 1, 2 * tq), F32),
                        pltpu.VMEM((hp, vd, 2 * tq), F32)],
        compiler_params=_cparams(("parallel", "parallel", "arbitrary")),
        name="diff",
    )(lam, proj, proj, proj, proj, ng)


def _lambda_kernel(p_ref, o_ref, *, lambda_init):
    p = p_ref[...]
    s1 = jnp.sum(p[0:1, :] * p[1:2, :], axis=-1, keepdims=True)
    s2 = jnp.sum(p[2:3, :] * p[3:4, :], axis=-1, keepdims=True)
    o_ref[...] = jnp.exp(s1) - jnp.exp(s2) + lambda_init


def _diff_lambda(lq1, lk1, lq2, lk2, lambda_init):
    p = jnp.stack([lq1, lk1, lq2, lk2]).astype(F32)
    out = pl.pallas_call(
        functools.partial(_lambda_kernel, lambda_init=lambda_init),
        out_shape=jax.ShapeDtypeStruct((1, 1), F32),
        name="diff_lambda",
    )(p)
    return out.reshape(1)


def _mix_ffn_kernel(x_ref, yc_ref, yg_ref, yd_ref, wo_ref, g_ref, wi_ref, wd_ref, fg_ref, o_ref, *, th, final):
    hidden = wd_ref.shape[0]
    y = (yc_ref[...].astype(F32) + yg_ref[...].astype(F32) + yd_ref[...].astype(F32)).astype(BF16)
    x1 = x_ref[...] + jnp.dot(y, wo_ref[...], preferred_element_type=F32)
    ms = jnp.mean(x1 * x1, axis=-1, keepdims=True)
    h = (x1 * lax.rsqrt(ms + EPS) * g_ref[...]).astype(BF16)
    x2 = x1
    for c in range(hidden // th):
        gate = jnp.dot(h, wi_ref[:, c * th:(c + 1) * th], preferred_element_type=F32)
        up = jnp.dot(h, wi_ref[:, hidden + c * th:hidden + (c + 1) * th], preferred_element_type=F32)
        act = (gate * _sigmoid(gate) * up).astype(BF16)
        x2 = x2 + jnp.dot(act, wd_ref[c * th:(c + 1) * th, :], preferred_element_type=F32)
    if final:
        ms = jnp.mean(x2 * x2, axis=-1, keepdims=True)
        x2 = x2 * lax.rsqrt(ms + EPS) * fg_ref[...]
    o_ref[...] = x2


def _mix_ffn(x2, yc, yg, yd, w_out, g, w_ffn_in, w_ffn_out, fg, *, tm, th, final):
    t, d = x2.shape
    tm = min(tm, t)
    kern = functools.partial(_mix_ffn_kernel, th=th, final=final)

    def tok():
        return pl.BlockSpec((tm, d), lambda i: (i, 0))

    def resident(shape):
        return pl.BlockSpec(shape, lambda i: (0, 0), pipeline_mode=pl.Buffered(1))

    return pl.pallas_call(
        kern,
        grid=(t // tm,),
        in_specs=[tok(), tok(), tok(), tok(), resident(w_out.shape), resident(g.shape),
                  resident(w_ffn_in.shape), resident(w_ffn_out.shape), resident(fg.shape)],
        out_specs=pl.BlockSpec((tm, d), lambda i: (i, 0)),
        out_shape=jax.ShapeDtypeStruct((t, d), F32),
        compiler_params=_cparams(("parallel",)),
        name="mix_ffn",
    )(x2, yc, yg, yd, w_out, g, w_ffn_in, w_ffn_out, fg)


def _regroup_in_proj(w, d, hdk, rank):
    widths = (d, d, hdk, hdk, d, d, rank, d, d, d, d, d, d)
    names = ("c_val", "c_gate", "g_q", "g_k", "g_v", "g_og", "g_f", "d_q", "d_k", "d_v", "m_conv", "m_gla", "m_diff")
    offs = [0]
    for wd in widths:
        offs.append(offs[-1] + wd)
    part = {n: w[:, offs[i]:offs[i + 1]] for i, n in enumerate(names)}
    order = ("c_val", "c_gate", "m_conv", "g_v", "g_og", "m_gla", "d_v", "m_diff", "d_q", "d_k", "g_q", "g_k")
    w_main = jnp.concatenate([part[n] for n in order], axis=1).astype(BF16)
    w_f = jnp.pad(part["g_f"], ((0, 0), (0, LANES - rank))).astype(BF16)
    return w_main, w_f


def kernel(x, norm_mix_g, w_in, conv_dw, conv_ln_g, conv_ln_b, w_conv_out, gla_wf_up, gla_bf, gla_norm_g,
           diff_lq1, diff_lk1, diff_lq2, diff_lk2, diff_norm_g, w_out, norm_ffn_g, w_ffn_in, w_ffn_out,
           final_norm_g):
    bsz, seq, d = x.shape
    depth = w_in.shape[0]
    hdk = gla_wf_up.shape[2]
    rank = gla_wf_up.shape[1]
    t = bsz * seq

    blk = {"c_val": 0, "c_gate": 1, "m_conv": 2, "g_v": 3, "g_og": 4, "m_gla": 5,
           "g_q": 10 * d // hdk, "g_k": 10 * d // hdk + 1}
    vd = diff_norm_g.shape[1]
    per = d // vd
    blk.update({"d_v": 6 * per, "m_diff": 7 * per, "d_q": 8 * per, "d_k": 9 * per})

    x2 = x.reshape(t, d)
    for l in range(depth):
        lambda_init = 0.8 - 0.6 * math.exp(-0.3 * l)
        w_main, w_f = _regroup_in_proj(w_in[l], d, hdk, rank)
        proj, f_low = _inproj(x2, norm_mix_g[l][None, :], w_main, w_f, **TILES["inproj"])

        y_conv = _conv(proj, conv_dw[l], conv_ln_g[l][None, :], conv_ln_b[l][None, :],
                       w_conv_out[l].astype(BF16), bsz=bsz, seq=seq, blk=blk, **TILES["conv"])

        wf_up = jnp.pad(gla_wf_up[l], ((0, LANES - rank), (0, 0))).astype(BF16)
        y_gla = _gla(proj, f_low, wf_up, gla_bf[l][None, :], gla_norm_g[l][None, :],
                     bsz=bsz, seq=seq, blk=blk, **TILES["gla"])

        lam = _diff_lambda(diff_lq1[l], diff_lk1[l], diff_lq2[l], diff_lk2[l], lambda_init)
        y_diff = _diff(lam, proj, diff_norm_g[l][None, :], bsz=bsz, seq=seq,
                       lambda_init=lambda_init, blk=blk, **TILES["diff"])

        x2 = _mix_ffn(x2, y_conv, y_gla, y_diff, w_out[l].astype(BF16), norm_ffn_g[l][None, :],
                      w_ffn_in[l].astype(BF16), w_ffn_out[l].astype(BF16), final_norm_g[None, :],
                      final=(l == depth - 1), **TILES["mix_ffn"])
    return x2.reshape(bsz, seq, d)
```

```python
import functools
import math

import jax
import jax.numpy as jnp
from jax import lax
from jax.experimental import pallas as pl
from jax.experimental.pallas import tpu as pltpu

F32 = jnp.float32
BF16 = jnp.bfloat16

EPS = 1e-6
CHUNK = 64
CONV_KERNEL = 31
CONV_HALO = 32
GLA_HEADS = 4
GLA_RANK = 16
GLA_GATE_NORM = 16.0
DIFF_HEADS = 8
LANES = 128
SUBLANES = 8
VMEM_LIMIT = 56 * 1024 * 1024
LOG2E = 1.4426950408889634
ONES_ROWS = 16

TILES = {
    "inproj": dict(tm=2048, tn=1024),
    "conv": dict(ts=512, rc=32, rb=128),
    "gla": dict(tg=512),
    "diff": dict(tq=1024, qs=256, hp=2),
    "mix_ffn": dict(tm=512, th=256),
}


def _cparams(sem):
    return pltpu.CompilerParams(dimension_semantics=sem, vmem_limit_bytes=VMEM_LIMIT)


def _sigmoid(x):
    return 1.0 / (1.0 + jnp.exp(-x))


def _log_sigmoid(x):
    return jnp.minimum(x, 0.0) - jnp.log(1.0 + jnp.exp(-jnp.abs(x)))


def _split_bf16(x):
    hi = x.astype(BF16)
    lo = (x - hi.astype(F32)).astype(BF16)
    return hi, lo


def _inproj_kernel(x_ref, g_ref, w_ref, wf_ref, o_ref, f_ref, h_scr):
    @pl.when(pl.program_id(1) == 0)
    def _():
        x = x_ref[...]
        ms = jnp.mean(x * x, axis=-1, keepdims=True)
        h = (x * lax.rsqrt(ms + EPS) * g_ref[...]).astype(BF16)
        h_scr[...] = h
        f_ref[...] = jnp.dot(h, wf_ref[...], preferred_element_type=F32)

    o_ref[...] = jnp.dot(h_scr[...], w_ref[...], preferred_element_type=F32).astype(BF16)


def _inproj(x2, g, w_main, w_f, *, tm, tn):
    t, d = x2.shape
    tm = min(tm, t)
    nw = w_main.shape[1]
    return pl.pallas_call(
        _inproj_kernel,
        grid=(t // tm, nw // tn),
        in_specs=[
            pl.BlockSpec((tm, d), lambda i, j: (i, 0)),
            pl.BlockSpec((1, d), lambda i, j: (0, 0)),
            pl.BlockSpec((d, tn), lambda i, j: (0, j)),
            pl.BlockSpec((d, LANES), lambda i, j: (0, 0)),
        ],
        out_specs=[
            pl.BlockSpec((tm, tn), lambda i, j: (i, j)),
            pl.BlockSpec((tm, LANES), lambda i, j: (i, 0)),
        ],
        out_shape=[
            jax.ShapeDtypeStruct((t, nw), BF16),
            jax.ShapeDtypeStruct((t, LANES), F32),
        ],
        scratch_shapes=[pltpu.VMEM((tm, d), BF16)],
        compiler_params=_cparams(("parallel", "arbitrary")),
        name="inproj",
    )(x2, g, w_main, w_f)


def _conv_kernel(a_ref, gt_ref, m_ref, wdw_ref, lng_ref, lnb_ref, wpw_ref, o_ref, u_scr, c_scr, v_scr,
                 *, ts, rc, rb):
    d = a_ref.shape[1]

    @pl.when(pl.program_id(1) == 0)
    def _():
        u_scr[0:CONV_HALO, :] = jnp.zeros((CONV_HALO, d), F32)

    def glu_body(r, carry):
        r0 = pl.multiple_of(r * rc, rc)
        a = a_ref[pl.ds(r0, rc), :].astype(F32)
        gt = gt_ref[pl.ds(r0, rc), :].astype(F32)
        u_scr[pl.ds(CONV_HALO + r0, rc), :] = a * _sigmoid(gt)
        return carry

    lax.fori_loop(0, ts // rc, glu_body, 0)

    first = CONV_HALO - (CONV_KERNEL - 1)

    def conv_body(i, carry):
        t0 = pl.multiple_of(i * rb, rb)
        for s in range(d // LANES):
            lanes = slice(s * LANES, (s + 1) * LANES)
            acc = None
            for r in range(SUBLANES):
                rows = rb + (SUBLANES if r else 0)
                part = None
                for j in range(first, first + CONV_KERNEL):
                    if j % SUBLANES != r:
                        continue
                    term = (u_scr[pl.ds(t0 + j - r, rows), lanes] * wdw_ref[j - first:j - first + 1, lanes])
                    part = term if part is None else part + term
                if part is None:
                    continue
                if r:
                    part = pltpu.roll(part, rows - r, 0)[0:rb]
                acc = part if acc is None else acc + part
            c_scr[pl.ds(t0, rb), lanes] = acc
        return carry

    lax.fori_loop(0, ts // rb, conv_body, 0)
    u_scr[0:CONV_HALO, :] = u_scr[ts:ts + CONV_HALO, :]

    lng = lng_ref[...]
    lnb = lnb_ref[...]

    def ln_body(r, carry):
        r0 = pl.multiple_of(r * rc, rc)
        x = c_scr[pl.ds(r0, rc), :]
        mu = jnp.mean(x, axis=-1, keepdims=True)
        xc = x - mu
        var = jnp.mean(xc * xc, axis=-1, keepdims=True)
        y = xc * lax.rsqrt(var + EPS) * lng + lnb
        v_scr[pl.ds(r0, rc), :] = (y * _sigmoid(y)).astype(BF16)
        return carry

    lax.fori_loop(0, ts // rc, ln_body, 0, unroll=4)

    o = jnp.dot(v_scr[...], wpw_ref[...], preferred_element_type=F32)
    o_ref[...] = (o * _sigmoid(m_ref[...].astype(F32))).astype(BF16)


def _conv(proj, wdw, lng, lnb, wpw, *, bsz, seq, ts, rc, rb, blk):
    d = wpw.shape[0]
    ts = min(ts, seq)
    nst = seq // ts
    kern = functools.partial(_conv_kernel, ts=ts, rc=rc, rb=min(rb, ts))

    def col(c):
        return pl.BlockSpec((ts, d), lambda b, s: (b * nst + s, c))

    def full(shape):
        return pl.BlockSpec(shape, lambda b, s: (0, 0))

    return pl.pallas_call(
        kern,
        grid=(bsz, nst),
        in_specs=[col(blk["c_val"]), col(blk["c_gate"]), col(blk["m_conv"]),
                  full(wdw.shape), full(lng.shape), full(lnb.shape), full(wpw.shape)],
        out_specs=pl.BlockSpec((ts, d), lambda b, s: (b * nst + s, 0)),
        out_shape=jax.ShapeDtypeStruct((bsz * seq, d), BF16),
        scratch_shapes=[pltpu.VMEM((CONV_HALO + ts, d), F32), pltpu.VMEM((ts, d), F32), pltpu.VMEM((ts, d), BF16)],
        compiler_params=_cparams(("parallel", "arbitrary")),
        name="conv",
    )(proj, proj, proj, wdw, lng, lnb, wpw)


def _gla_kernel(q_ref, k_ref, v_ref, og_ref, m_ref, f_ref, wf_ref, bf_ref, ng_ref, o_ref,
                st_scr, qg_scr, kg_scr, kd_scr, *, tg, dk, dv):
    nc = tg // CHUNK

    @pl.when(pl.program_id(1) == 0)
    def _():
        st_scr[...] = jnp.zeros(st_scr.shape, F32)

    row = lax.broadcasted_iota(jnp.int32, (tg, tg), 0)
    col = lax.broadcasted_iota(jnp.int32, (tg, tg), 1)
    same = jnp.where(row // CHUNK == col // CHUNK, 1.0, 0.0)
    causal = jnp.where(row >= col, same, 0.0)
    chunk_cumsum = causal.astype(BF16)

    f_hi, f_lo = _split_bf16(f_ref[...])
    wf = wf_ref[...]
    logit = (jnp.dot(f_hi, wf, preferred_element_type=F32)
             + jnp.dot(f_lo, wf, preferred_element_type=F32) + bf_ref[...])
    logf = _log_sigmoid(logit) * (1.0 / GLA_GATE_NORM)
    l_hi, l_lo = _split_bf16(logf)
    g = (jnp.dot(chunk_cumsum, l_hi, preferred_element_type=F32)
         + jnp.dot(chunk_cumsum, l_lo, preferred_element_type=F32))
    last = [g[(c + 1) * CHUNK - 1:(c + 1) * CHUNK, :] for c in range(nc)]
    g_last = jnp.concatenate([jnp.broadcast_to(r, (CHUNK, r.shape[1])) for r in last], axis=0)
    q = q_ref[...].astype(F32) * (dk ** -0.5)
    k = k_ref[...].astype(F32)
    qg_scr[...] = (q * jnp.exp(g)).astype(BF16)
    kg_scr[...] = (k * jnp.exp(-g)).astype(BF16)
    kd_scr[...] = (k * jnp.exp(g_last - g)).astype(BF16)
    decay = [jnp.exp(r) for r in last]
    ng = ng_ref[...]

    for h in range(GLA_HEADS):
        ks = slice(h * dk, (h + 1) * dk)
        vs = slice(h * dv, (h + 1) * dv)
        att = lax.dot_general(qg_scr[:, ks], kg_scr[:, ks], (((1,), (1,)), ((), ())),
                              preferred_element_type=F32)
        att = jnp.where(causal > 0.0, att, 0.0).astype(BF16)
        o = jnp.dot(att, v_ref[:, vs], preferred_element_type=F32)
        st = st_scr[h]
        inter = []
        for c in range(nc):
            rows = slice(c * CHUNK, (c + 1) * CHUNK)
            inter.append(lax.dot_general(qg_scr[rows, ks], st.astype(BF16), (((1,), (1,)), ((), ())),
                                         preferred_element_type=F32))
            st = st * decay[c][:, ks] + lax.dot_general(
                v_ref[rows, vs], kd_scr[rows, ks], (((0,), (0,)), ((), ())), preferred_element_type=F32)
        st_scr[h] = st
        o = o + jnp.concatenate(inter, axis=0)
        o = o * lax.rsqrt(jnp.mean(o * o, axis=-1, keepdims=True) + EPS) * ng
        og = og_ref[:, vs].astype(F32)
        gate = _sigmoid(m_ref[:, vs].astype(F32))
        o_ref[:, vs] = (gate * (o * (og * _sigmoid(og)))).astype(BF16)


def _gla(proj, f_low, wf_up, bf, ng, *, bsz, seq, tg, blk):
    hdk = wf_up.shape[1]
    dk = hdk // GLA_HEADS
    dv = ng.shape[1]
    hdv = GLA_HEADS * dv
    tg = min(tg, seq)
    nst = seq // tg
    kern = functools.partial(_gla_kernel, tg=tg, dk=dk, dv=dv)

    def col(width, c):
        return pl.BlockSpec((tg, width), lambda b, s: (b * nst + s, c))

    def full(shape):
        return pl.BlockSpec(shape, lambda b, s: (0, 0))

    return pl.pallas_call(
        kern,
        grid=(bsz, nst),
        in_specs=[col(hdk, blk["g_q"]), col(hdk, blk["g_k"]), col(hdv, blk["g_v"]), col(hdv, blk["g_og"]),
                  col(hdv, blk["m_gla"]), col(LANES, 0),
                  full(wf_up.shape), full(bf.shape), full(ng.shape)],
        out_specs=pl.BlockSpec((tg, hdv), lambda b, s: (b * nst + s, 0)),
        out_shape=jax.ShapeDtypeStruct((bsz * seq, hdv), BF16),
        scratch_shapes=[pltpu.VMEM((GLA_HEADS, dv, dk), F32)] + [pltpu.VMEM((tg, hdk), BF16)] * 3,
        compiler_params=_cparams(("parallel", "arbitrary")),
        name="gla",
    )(proj, proj, proj, proj, proj, f_low, wf_up, bf, ng)


def _diff_kernel(lam_ref, q_ref, k_ref, v_ref, m_ref, ng_ref, o_ref, vt_scr, qq_scr, m_scr, acc_scr, s_scr,
                 *, tq, qs, hp, hd, lambda_init):
    qi = pl.program_id(2)
    seq = k_ref.shape[0]
    vd = 2 * hd
    heads = [slice(h * vd, (h + 1) * vd) for h in range(hp)]
    strips = [(h, c) for h in range(hp) for c in range(2 * tq // qs)]

    @pl.when(qi == 0)
    def _():
        for h in range(hp):
            for c in range(seq // tq):
                rows = slice(c * tq, (c + 1) * tq)
                vt_scr[h, 0:vd, rows] = v_ref[rows, heads[h]].astype(F32).T.astype(BF16)
            vt_scr[h, vd:vd + ONES_ROWS, :] = jnp.ones((ONES_ROWS, seq), BF16)

    for h in range(hp):
        q = q_ref[:, heads[h]].astype(F32) * (hd ** -0.5 * LOG2E)
        lane = lax.broadcasted_iota(jnp.int32, q.shape, 1)
        qq_scr[h, 0:tq, :] = jnp.where(lane < hd, q, 0.0).astype(BF16)
        qq_scr[h, tq:2 * tq, :] = jnp.where(lane >= hd, q, 0.0).astype(BF16)

    m_scr[...] = jnp.full(m_scr.shape, -jnp.inf, F32)
    acc_scr[...] = jnp.zeros(acc_scr.shape, F32)

    def step(j, masked):
        c0 = pl.multiple_of(j * tq, tq)

        def keys(c):
            q0 = (c * qs) % tq
            return q0, (min(tq, q0 + qs) if masked else tq)

        for i, (h, c) in enumerate(strips):
            q0, nk = keys(c)
            st = lax.dot_general(k_ref[pl.ds(c0, nk), heads[h]], qq_scr[h, c * qs:(c + 1) * qs, :],
                                 (((1,), (1,)), ((), ())), preferred_element_type=F32)
            if masked:
                kc = lax.broadcasted_iota(jnp.int32, st.shape, 0) // CHUNK
                qc = (lax.broadcasted_iota(jnp.int32, st.shape, 1) + q0) // CHUNK
                st = jnp.where(kc <= qc, st, -jnp.inf)
            s_scr[i, 0:nk, :] = st
        for i, (h, c) in enumerate(strips):
            cols = slice(c * qs, (c + 1) * qs)
            q0, nk = keys(c)
            m_prev = m_scr[h, :, cols]
            m_new = jnp.maximum(m_prev, jnp.max(s_scr[i, 0:nk, :], axis=0, keepdims=True))
            alpha = jnp.exp2(m_prev - m_new)
            pt = jnp.exp2(s_scr[i, 0:nk, :] - m_new)
            acc_scr[h, :, cols] = alpha * acc_scr[h, :, cols] + jnp.dot(
                vt_scr[h, :, pl.ds(c0, nk)], pt.astype(BF16), preferred_element_type=F32)
            m_scr[h, :, cols] = m_new

    def body(j, carry):
        step(j, False)
        return carry

    lax.fori_loop(0, qi, body, 0)
    step(qi, True)

    lam = lam_ref[0]
    for h in range(hp):
        ot = acc_scr[h, 0:vd, :] / acc_scr[h, vd:vd + 1, :]
        od = (ot[:, 0:tq] - lam * ot[:, tq:2 * tq]).T
        od = od * lax.rsqrt(jnp.mean(od * od, axis=-1, keepdims=True) + EPS) * ng_ref[...] * (1.0 - lambda_init)
        o_ref[:, heads[h]] = (od * _sigmoid(m_ref[:, heads[h]].astype(F32))).astype(BF16)


def _diff(lam, proj, ng, *, bsz, seq, tq, qs, hp, lambda_init, blk):
    vd = ng.shape[1]
    hd = vd // 2
    tq = min(tq, seq)
    nq = seq // tq
    w = hp * vd
    kern = functools.partial(_diff_kernel, tq=tq, qs=qs, hp=hp, hd=hd, lambda_init=lambda_init)
    return pl.pallas_call(
        kern,
        grid=(bsz, DIFF_HEADS // hp, nq),
        in_specs=[
            pl.BlockSpec(memory_space=pltpu.SMEM),
            pl.BlockSpec((tq, w), lambda b, h, i: (b * nq + i, blk["d_q"] // hp + h)),
            pl.BlockSpec((seq, w), lambda b, h, i: (b, blk["d_k"] // hp + h)),
            pl.BlockSpec((seq, w), lambda b, h, i: (b, blk["d_v"] // hp + h)),
            pl.BlockSpec((tq, w), lambda b, h, i: (b * nq + i, blk["m_diff"] // hp + h)),
            pl.BlockSpec((1, vd), lambda b, h, i: (0, 0)),
        ],
        out_specs=pl.BlockSpec((tq, w), lambda b, h, i: (b * nq + i, h)),
        out_shape=jax.ShapeDtypeStruct((bsz * seq, DIFF_HEADS * vd), BF16),
        scratch_shapes=[pltpu.VMEM((hp, vd + ONES_ROWS, seq), BF16), pltpu.VMEM((hp, 2 * tq, vd), BF16),
                        pltpu.VMEM((hp, 1, 2 * tq), F32), pltpu.VMEM((hp, vd + ONES_ROWS, 2 * tq), F32),
                        pltpu.VMEM((hp * 2 * tq // qs, tq, qs), F32)],
        compiler_params=_cparams(("parallel", "parallel", "arbitrary")),
        name="diff",
    )(lam, proj, proj, proj, proj, ng)


def _lambda_kernel(p_ref, o_ref, *, lambda_init):
    p = p_ref[...]
    s1 = jnp.sum(p[0:1, :] * p[1:2, :], axis=-1, keepdims=True)
    s2 = jnp.sum(p[2:3, :] * p[3:4, :], axis=-1, keepdims=True)
    o_ref[...] = jnp.exp(s1) - jnp.exp(s2) + lambda_init


def _diff_lambda(lq1, lk1, lq2, lk2, lambda_init):
    p = jnp.stack([lq1, lk1, lq2, lk2]).astype(F32)
    out = pl.pallas_call(
        functools.partial(_lambda_kernel, lambda_init=lambda_init),
        out_shape=jax.ShapeDtypeStruct((1, 1), F32),
        name="diff_lambda",
    )(p)
    return out.reshape(1)


def _mix_ffn_kernel(x_ref, yc_ref, yg_ref, yd_ref, wo_ref, g_ref, wi_ref, wd_ref, fg_ref, o_ref, *, th, final):
    hidden = wd_ref.shape[0]
    y = (yc_ref[...].astype(F32) + yg_ref[...].astype(F32) + yd_ref[...].astype(F32)).astype(BF16)
    x1 = x_ref[...] + jnp.dot(y, wo_ref[...], preferred_element_type=F32)
    ms = jnp.mean(x1 * x1, axis=-1, keepdims=True)
    h = (x1 * lax.rsqrt(ms + EPS) * g_ref[...]).astype(BF16)
    x2 = x1
    for c in range(hidden // th):
        gate = jnp.dot(h, wi_ref[:, c * th:(c + 1) * th], preferred_element_type=F32)
        up = jnp.dot(h, wi_ref[:, hidden + c * th:hidden + (c + 1) * th], preferred_element_type=F32)
        act = (gate * _sigmoid(gate) * up).astype(BF16)
        x2 = x2 + jnp.dot(act, wd_ref[c * th:(c + 1) * th, :], preferred_element_type=F32)
    if final:
        ms = jnp.mean(x2 * x2, axis=-1, keepdims=True)
        x2 = x2 * lax.rsqrt(ms + EPS) * fg_ref[...]
    o_ref[...] = x2


def _mix_ffn(x2, yc, yg, yd, w_out, g, w_ffn_in, w_ffn_out, fg, *, tm, th, final):
    t, d = x2.shape
    tm = min(tm, t)
    kern = functools.partial(_mix_ffn_kernel, th=th, final=final)

    def tok():
        return pl.BlockSpec((tm, d), lambda i: (i, 0))

    def resident(shape):
        return pl.BlockSpec(shape, lambda i: (0, 0), pipeline_mode=pl.Buffered(1))

    return pl.pallas_call(
        kern,
        grid=(t // tm,),
        in_specs=[tok(), tok(), tok(), tok(), resident(w_out.shape), resident(g.shape),
                  resident(w_ffn_in.shape), resident(w_ffn_out.shape), resident(fg.shape)],
        out_specs=pl.BlockSpec((tm, d), lambda i: (i, 0)),
        out_shape=jax.ShapeDtypeStruct((t, d), F32),
        compiler_params=_cparams(("parallel",)),
        name="mix_ffn",
    )(x2, yc, yg, yd, w_out, g, w_ffn_in, w_ffn_out, fg)


def _regroup_in_proj(w, d, hdk, rank):
    widths = (d, d, hdk, hdk, d, d, rank, d, d, d, d, d, d)
    names = ("c_val", "c_gate", "g_q", "g_k", "g_v", "g_og", "g_f", "d_q", "d_k", "d_v", "m_conv", "m_gla", "m_diff")
    offs = [0]
    for wd in widths:
        offs.append(offs[-1] + wd)
    part = {n: w[:, offs[i]:offs[i + 1]] for i, n in enumerate(names)}
    order = ("c_val", "c_gate", "m_conv", "g_v", "g_og", "m_gla", "d_v", "m_diff", "d_q", "d_k", "g_q", "g_k")
    w_main = jnp.concatenate([part[n] for n in order], axis=1).astype(BF16)
    w_f = jnp.pad(part["g_f"], ((0, 0), (0, LANES - rank))).astype(BF16)
    return w_main, w_f


def kernel(x, norm_mix_g, w_in, conv_dw, conv_ln_g, conv_ln_b, w_conv_out, gla_wf_up, gla_bf, gla_norm_g,
           diff_lq1, diff_lk1, diff_lq2, diff_lk2, diff_norm_g, w_out, norm_ffn_g, w_ffn_in, w_ffn_out,
           final_norm_g):
    bsz, seq, d = x.shape
    depth = w_in.shape[0]
    hdk = gla_wf_up.shape[2]
    rank = gla_wf_up.shape[1]
    t = bsz * seq

    blk = {"c_val": 0, "c_gate": 1, "m_conv": 2, "g_v": 3, "g_og": 4, "m_gla": 5,
           "g_q": 10 * d // hdk, "g_k": 10 * d // hdk + 1}
    vd = diff_norm_g.shape[1]
    per = d // vd
    blk.update({"d_v": 6 * per, "m_diff": 7 * per, "d_q": 8 * per, "d_k": 9 * per})

    x2 = x.reshape(t, d)
    for l in range(depth):
        lambda_init = 0.8 - 0.6 * math.exp(-0.3 * l)
        w_main, w_f = _regroup_in_proj(w_in[l], d, hdk, rank)
        proj, f_low = _inproj(x2, norm_mix_g[l][None, :], w_main, w_f, **TILES["inproj"])

        y_conv = _conv(proj, conv_dw[l], conv_ln_g[l][None, :], conv_ln_b[l][None, :],
                       w_conv_out[l].astype(BF16), bsz=bsz, seq=seq, blk=blk, **TILES["conv"])

        wf_up = jnp.pad(gla_wf_up[l], ((0, LANES - rank), (0, 0))).astype(BF16)
        y_gla = _gla(proj, f_low, wf_up, gla_bf[l][None, :], gla_norm_g[l][None, :],
                     bsz=bsz, seq=seq, blk=blk, **TILES["gla"])

        lam = _diff_lambda(diff_lq1[l], diff_lk1[l], diff_lq2[l], diff_lk2[l], lambda_init)
        y_diff = _diff(lam, proj, diff_norm_g[l][None, :], bsz=bsz, seq=seq,
                       lambda_init=lambda_init, blk=blk, **TILES["diff"])

        x2 = _mix_ffn(x2, y_conv, y_gla, y_diff, w_out[l].astype(BF16), norm_ffn_g[l][None, :],
                      w_ffn_in[l].astype(BF16), w_ffn_out[l].astype(BF16), final_norm_g[None, :],
                      final=(l == depth - 1), **TILES["mix_ffn"])
    return x2.reshape(bsz, seq, d)
```

```python
import functools
import math

import jax
import jax.numpy as jnp
from jax import lax
from jax.experimental import pallas as pl
from jax.experimental.pallas import tpu as pltpu

F32 = jnp.float32
BF16 = jnp.bfloat16

EPS = 1e-6
CHUNK = 64
CONV_KERNEL = 31
CONV_HALO = 32
GLA_HEADS = 4
GLA_RANK = 16
GLA_GATE_NORM = 16.0
DIFF_HEADS = 8
LANES = 128
SUBLANES = 8
VMEM_LIMIT = 56 * 1024 * 1024
LOG2E = 1.4426950408889634
ONES_ROWS = 16

TILES = {
    "inproj": dict(tm=2048, tn=1024),
    "conv": dict(ts=512, rc=32, rb=128),
    "gla": dict(tg=512),
    "diff": dict(tq=1024, qs=256, hp=2),
    "mix_ffn": dict(tm=512, th=256),
}


def _cparams(sem):
    return pltpu.CompilerParams(dimension_semantics=sem, vmem_limit_bytes=VMEM_LIMIT)


def _sigmoid(x):
    return 1.0 / (1.0 + jnp.exp(-x))


def _log_sigmoid(x):
    return jnp.minimum(x, 0.0) - jnp.log(1.0 + jnp.exp(-jnp.abs(x)))


def _split_bf16(x):
    hi = x.astype(BF16)
    lo = (x - hi.astype(F32)).astype(BF16)
    return hi, lo


def _inproj_kernel(x_ref, g_ref, w_ref, wf_ref, o_ref, f_ref, h_scr):
    @pl.when(pl.program_id(1) == 0)
    def _():
        x = x_ref[...]
        ms = jnp.mean(x * x, axis=-1, keepdims=True)
        h = (x * lax.rsqrt(ms + EPS) * g_ref[...]).astype(BF16)
        h_scr[...] = h
        f_ref[...] = jnp.dot(h, wf_ref[...], preferred_element_type=F32)

    o_ref[...] = jnp.dot(h_scr[...], w_ref[...], preferred_element_type=F32).astype(BF16)


def _layer(l, *block):
    return pl.BlockSpec((None,) + block, lambda *_: (l,) + (0,) * len(block))


def _inproj(l, x2, g, w_main, w_f, *, tm, tn):
    t, d = x2.shape
    tm = min(tm, t)
    nw = w_main.shape[2]
    return pl.pallas_call(
        _inproj_kernel,
        grid=(t // tm, nw // tn),
        in_specs=[
            pl.BlockSpec((tm, d), lambda i, j: (i, 0)),
            _layer(l, 1, d),
            pl.BlockSpec((None, d, tn), lambda i, j: (l, 0, j)),
            _layer(l, d, LANES),
        ],
        out_specs=[
            pl.BlockSpec((tm, tn), lambda i, j: (i, j)),
            pl.BlockSpec((tm, LANES), lambda i, j: (i, 0)),
        ],
        out_shape=[
            jax.ShapeDtypeStruct((t, nw), BF16),
            jax.ShapeDtypeStruct((t, LANES), F32),
        ],
        scratch_shapes=[pltpu.VMEM((tm, d), BF16)],
        compiler_params=_cparams(("parallel", "arbitrary")),
        name="inproj",
    )(x2, g, w_main, w_f)


def _conv_kernel(a_ref, gt_ref, m_ref, wdw_ref, lng_ref, lnb_ref, wpw_ref, o_ref, u_scr, c_scr, v_scr,
                 *, ts, rc, rb):
    d = a_ref.shape[1]

    @pl.when(pl.program_id(1) == 0)
    def _():
        u_scr[0:CONV_HALO, :] = jnp.zeros((CONV_HALO, d), F32)

    def glu_body(r, carry):
        r0 = pl.multiple_of(r * rc, rc)
        a = a_ref[pl.ds(r0, rc), :].astype(F32)
        gt = gt_ref[pl.ds(r0, rc), :].astype(F32)
        u_scr[pl.ds(CONV_HALO + r0, rc), :] = a * _sigmoid(gt)
        return carry

    lax.fori_loop(0, ts // rc, glu_body, 0)

    first = CONV_HALO - (CONV_KERNEL - 1)

    def conv_body(i, carry):
        t0 = pl.multiple_of(i * rb, rb)
        for s in range(d // LANES):
            lanes = slice(s * LANES, (s + 1) * LANES)
            acc = None
            for r in range(SUBLANES):
                rows = rb + (SUBLANES if r else 0)
                part = None
                for j in range(first, first + CONV_KERNEL):
                    if j % SUBLANES != r:
                        continue
                    term = (u_scr[pl.ds(t0 + j - r, rows), lanes] * wdw_ref[j - first:j - first + 1, lanes])
                    part = term if part is None else part + term
                if part is None:
                    continue
                if r:
                    part = pltpu.roll(part, rows - r, 0)[0:rb]
                acc = part if acc is None else acc + part
            c_scr[pl.ds(t0, rb), lanes] = acc
        return carry

    lax.fori_loop(0, ts // rb, conv_body, 0)
    u_scr[0:CONV_HALO, :] = u_scr[ts:ts + CONV_HALO, :]

    lng = lng_ref[...]
    lnb = lnb_ref[...]

    def ln_body(r, carry):
        r0 = pl.multiple_of(r * rc, rc)
        x = c_scr[pl.ds(r0, rc), :]
        mu = jnp.mean(x, axis=-1, keepdims=True)
        xc = x - mu
        var = jnp.mean(xc * xc, axis=-1, keepdims=True)
        y = xc * lax.rsqrt(var + EPS) * lng + lnb
        v_scr[pl.ds(r0, rc), :] = (y * _sigmoid(y)).astype(BF16)
        return carry

    lax.fori_loop(0, ts // rc, ln_body, 0, unroll=4)

    o = jnp.dot(v_scr[...], wpw_ref[...], preferred_element_type=F32)
    o_ref[...] = (o * _sigmoid(m_ref[...].astype(F32))).astype(BF16)


def _conv(l, proj, wdw, lng, lnb, wpw, *, bsz, seq, ts, rc, rb, blk):
    d = wpw.shape[1]
    ts = min(ts, seq)
    nst = seq // ts
    kern = functools.partial(_conv_kernel, ts=ts, rc=rc, rb=min(rb, ts))

    def col(c):
        return pl.BlockSpec((ts, d), lambda b, s: (b * nst + s, c))

    return pl.pallas_call(
        kern,
        grid=(bsz, nst),
        in_specs=[col(blk["c_val"]), col(blk["c_gate"]), col(blk["m_conv"]),
                  _layer(l, *wdw.shape[1:]), _layer(l, 1, d), _layer(l, 1, d), _layer(l, d, d)],
        out_specs=pl.BlockSpec((ts, d), lambda b, s: (b * nst + s, 0)),
        out_shape=jax.ShapeDtypeStruct((bsz * seq, d), BF16),
        scratch_shapes=[pltpu.VMEM((CONV_HALO + ts, d), F32), pltpu.VMEM((ts, d), F32), pltpu.VMEM((ts, d), BF16)],
        compiler_params=_cparams(("parallel", "arbitrary")),
        name="conv",
    )(proj, proj, proj, wdw, lng, lnb, wpw)


def _gla_kernel(q_ref, k_ref, v_ref, og_ref, m_ref, f_ref, wf_ref, bf_ref, ng_ref, o_ref,
                st_scr, qg_scr, kg_scr, kd_scr, causal_scr, cumsum_scr, g_scr, o_scr, *, tg, dk, dv):
    nc = tg // CHUNK

    @pl.when(pl.program_id(1) == 0)
    def _():
        st_scr[...] = jnp.zeros(st_scr.shape, F32)
        row = lax.broadcasted_iota(jnp.int32, (tg, tg), 0)
        col = lax.broadcasted_iota(jnp.int32, (tg, tg), 1)
        same = jnp.where(row // CHUNK == col // CHUNK, 1.0, 0.0)
        causal = jnp.where(row >= col, same, 0.0)
        causal_scr[...] = causal
        cumsum_scr[...] = causal.astype(BF16)

    chunk_cumsum = cumsum_scr[...]

    f_hi, f_lo = _split_bf16(f_ref[...])
    wf = wf_ref[...]
    logit = (jnp.dot(f_hi, wf, preferred_element_type=F32)
             + jnp.dot(f_lo, wf, preferred_element_type=F32) + bf_ref[...])
    logf = _log_sigmoid(logit) * (1.0 / GLA_GATE_NORM)
    l_hi, l_lo = _split_bf16(logf)
    g_scr[...] = (jnp.dot(chunk_cumsum, l_hi, preferred_element_type=F32)
                  + jnp.dot(chunk_cumsum, l_lo, preferred_element_type=F32))
    last = [g_scr[(c + 1) * CHUNK - 1:(c + 1) * CHUNK, :] for c in range(nc)]
    qg_scr[...] = (q_ref[...].astype(F32) * (dk ** -0.5) * jnp.exp(g_scr[...])).astype(BF16)
    kg_scr[...] = (k_ref[...].astype(F32) * jnp.exp(-g_scr[...])).astype(BF16)
    for c in range(nc):
        rows = slice(c * CHUNK, (c + 1) * CHUNK)
        kd_scr[rows, :] = (k_ref[rows, :].astype(F32) * jnp.exp(last[c] - g_scr[rows, :])).astype(BF16)
    decay = [jnp.exp(r) for r in last]
    ng = ng_ref[...]

    for h in range(GLA_HEADS):
        ks = slice(h * dk, (h + 1) * dk)
        vs = slice(h * dv, (h + 1) * dv)
        att = lax.dot_general(qg_scr[:, ks], kg_scr[:, ks], (((1,), (1,)), ((), ())),
                              preferred_element_type=F32)
        att = jnp.where(causal_scr[...] > 0.0, att, 0.0).astype(BF16)
        o_scr[...] = jnp.dot(att, v_ref[:, vs], preferred_element_type=F32)
        st = st_scr[h]
        for c in range(nc):
            rows = slice(c * CHUNK, (c + 1) * CHUNK)
            o_scr[rows, :] += lax.dot_general(qg_scr[rows, ks], st.astype(BF16), (((1,), (1,)), ((), ())),
                                              preferred_element_type=F32)
            st = st * decay[c][:, ks] + lax.dot_general(
                v_ref[rows, vs], kd_scr[rows, ks], (((0,), (0,)), ((), ())), preferred_element_type=F32)
        st_scr[h] = st
        o = o_scr[...]
        o = o * lax.rsqrt(jnp.mean(o * o, axis=-1, keepdims=True) + EPS) * ng
        og = og_ref[:, vs].astype(F32)
        gate = _sigmoid(m_ref[:, vs].astype(F32))
        o_ref[:, vs] = (gate * (o * (og * _sigmoid(og)))).astype(BF16)


def _gla(l, proj, f_low, wf_up, bf, ng, *, bsz, seq, tg, blk):
    hdk = wf_up.shape[2]
    dk = hdk // GLA_HEADS
    dv = ng.shape[2]
    hdv = GLA_HEADS * dv
    tg = min(tg, seq)
    nst = seq // tg
    kern = functools.partial(_gla_kernel, tg=tg, dk=dk, dv=dv)

    def col(width, c):
        return pl.BlockSpec((tg, width), lambda b, s: (b * nst + s, c))

    return pl.pallas_call(
        kern,
        grid=(bsz, nst),
        in_specs=[col(hdk, blk["g_q"]), col(hdk, blk["g_k"]), col(hdv, blk["g_v"]), col(hdv, blk["g_og"]),
                  col(hdv, blk["m_gla"]), col(LANES, 0),
                  _layer(l, LANES, hdk), _layer(l, 1, hdk), _layer(l, 1, dv)],
        out_specs=pl.BlockSpec((tg, hdv), lambda b, s: (b * nst + s, 0)),
        out_shape=jax.ShapeDtypeStruct((bsz * seq, hdv), BF16),
        scratch_shapes=([pltpu.VMEM((GLA_HEADS, dv, dk), F32)] + [pltpu.VMEM((tg, hdk), BF16)] * 3
                        + [pltpu.VMEM((tg, tg), F32), pltpu.VMEM((tg, tg), BF16),
                           pltpu.VMEM((tg, hdk), F32), pltpu.VMEM((tg, dv), F32)]),
        compiler_params=_cparams(("parallel", "arbitrary")),
        name="gla",
    )(proj, proj, proj, proj, proj, f_low, wf_up, bf, ng)


def _diff_kernel(lam_ref, q_ref, k_ref, v_ref, m_ref, ng_ref, o_ref, vt_scr, qq_scr, m_scr, acc_scr, s_scr,
                 *, tq, qs, hp, hd, lambda_init):
    qi = pl.program_id(2)
    seq = k_ref.shape[0]
    vd = 2 * hd
    heads = [slice(h * vd, (h + 1) * vd) for h in range(hp)]
    strips = [(h, c) for h in range(hp) for c in range(2 * tq // qs)]

    @pl.when(qi == 0)
    def _():
        for h in range(hp):
            for c in range(seq // tq):
                rows = slice(c * tq, (c + 1) * tq)
                vt_scr[h, 0:vd, rows] = v_ref[rows, heads[h]].astype(F32).T.astype(BF16)
            vt_scr[h, vd:vd + ONES_ROWS, :] = jnp.ones((ONES_ROWS, seq), BF16)

    for h in range(hp):
        q = q_ref[:, heads[h]].astype(F32) * (hd ** -0.5 * LOG2E)
        lane = lax.broadcasted_iota(jnp.int32, q.shape, 1)
        qq_scr[h, 0:tq, :] = jnp.where(lane < hd, q, 0.0).astype(BF16)
        qq_scr[h, tq:2 * tq, :] = jnp.where(lane >= hd, q, 0.0).astype(BF16)

    m_scr[...] = jnp.full(m_scr.shape, -jnp.inf, F32)
    acc_scr[...] = jnp.zeros(acc_scr.shape, F32)

    def step(j, masked):
        c0 = pl.multiple_of(j * tq, tq)

        def keys(c):
            q0 = (c * qs) % tq
            return q0, (min(tq, q0 + qs) if masked else tq)

        for i, (h, c) in enumerate(strips):
            q0, nk = keys(c)
            st = lax.dot_general(k_ref[pl.ds(c0, nk), heads[h]], qq_scr[h, c * qs:(c + 1) * qs, :],
                                 (((1,), (1,)), ((), ())), preferred_element_type=F32)
            if masked:
                kc = lax.broadcasted_iota(jnp.int32, st.shape, 0) // CHUNK
                qc = (lax.broadcasted_iota(jnp.int32, st.shape, 1) + q0) // CHUNK
                st = jnp.where(kc <= qc, st, -jnp.inf)
            s_scr[i, 0:nk, :] = st
        for i, (h, c) in enumerate(strips):
            cols = slice(c * qs, (c + 1) * qs)
            q0, nk = keys(c)
            m_prev = m_scr[h, :, cols]
            m_new = jnp.maximum(m_prev, jnp.max(s_scr[i, 0:nk, :], axis=0, keepdims=True))
            alpha = jnp.exp2(m_prev - m_new)
            pt = jnp.exp2(s_scr[i, 0:nk, :] - m_new)
            acc_scr[h, :, cols] = alpha * acc_scr[h, :, cols] + jnp.dot(
                vt_scr[h, :, pl.ds(c0, nk)], pt.astype(BF16), preferred_element_type=F32)
            m_scr[h, :, cols] = m_new

    def body(j, carry):
        step(j, False)
        return carry

    lax.fori_loop(0, qi, body, 0)
    step(qi, True)

    lam = lam_ref[0]
    for h in range(hp):
        ot = acc_scr[h, 0:vd, :] / acc_scr[h, vd:vd + 1, :]
        od = (ot[:, 0:tq] - lam * ot[:, tq:2 * tq]).T
        od = od * lax.rsqrt(jnp.mean(od * od, axis=-1, keepdims=True) + EPS) * ng_ref[...] * (1.0 - lambda_init)
        o_ref[:, heads[h]] = (od * _sigmoid(m_ref[:, heads[h]].astype(F32))).astype(BF16)


def _diff(l, lam, proj, ng, *, bsz, seq, tq, qs, hp, lambda_init, blk):
    vd = ng.shape[2]
    hd = vd // 2
    tq = min(tq, seq)
    nq = seq // tq
    w = hp * vd
    kern = functools.partial(_diff_kernel, tq=tq, qs=qs, hp=hp, hd=hd, lambda_init=lambda_init)
    return pl.pallas_call(
        kern,
        grid=(bsz, DIFF_HEADS // hp, nq),
        in_specs=[
            pl.BlockSpec(memory_space=pltpu.SMEM),
            pl.BlockSpec((tq, w), lambda b, h, i: (b * nq + i, blk["d_q"] + h)),
            pl.BlockSpec((seq, w), lambda b, h, i: (b, blk["d_k"] + h)),
            pl.BlockSpec((seq, w), lambda b, h, i: (b, blk["d_v"] + h)),
            pl.BlockSpec((tq, w), lambda b, h, i: (b * nq + i, blk["m_diff"] + h)),
            _layer(l, 1, vd),
        ],
        out_specs=pl.BlockSpec((tq, w), lambda b, h, i: (b * nq + i, h)),
        out_shape=jax.ShapeDtypeStruct((bsz * seq, DIFF_HEADS * vd), BF16),
        scratch_shapes=[pltpu.VMEM((hp, vd + ONES_ROWS, seq), BF16), pltpu.VMEM((hp, 2 * tq, vd), BF16),
                        pltpu.VMEM((hp, 1, 2 * tq), F32), pltpu.VMEM((hp, vd + ONES_ROWS, 2 * tq), F32),
                        pltpu.VMEM((hp * 2 * tq // qs, tq, qs), F32)],
        compiler_params=_cparams(("parallel", "parallel", "arbitrary")),
        name="diff",
    )(lam, proj, proj, proj, proj, ng)


def _lambda_kernel(p_ref, o_ref, *, lambda_init):
    p = p_ref[...]
    s1 = jnp.sum(p[0:1, :] * p[1:2, :], axis=-1, keepdims=True)
    s2 = jnp.sum(p[2:3, :] * p[3:4, :], axis=-1, keepdims=True)
    o_ref[...] = jnp.exp(s1) - jnp.exp(s2) + lambda_init


def _diff_lambda(l, p, lambda_init):
    out = pl.pallas_call(
        functools.partial(_lambda_kernel, lambda_init=lambda_init),
        grid=(1,),
        in_specs=[_layer(l, *p.shape[1:])],
        out_specs=pl.BlockSpec((1, 1), lambda i: (0, 0)),
        out_shape=jax.ShapeDtypeStruct((1, 1), F32),
        name="diff_lambda",
    )(p)
    return out.reshape(1)


def _mix_ffn_kernel(x_ref, yc_ref, yg_ref, yd_ref, wo_ref, g_ref, wi_ref, wd_ref, fg_ref, o_ref, *, th, final):
    hidden = wd_ref.shape[0]
    y = (yc_ref[...].astype(F32) + yg_ref[...].astype(F32) + yd_ref[...].astype(F32)).astype(BF16)
    x1 = x_ref[...] + jnp.dot(y, wo_ref[...], preferred_element_type=F32)
    ms = jnp.mean(x1 * x1, axis=-1, keepdims=True)
    h = (x1 * lax.rsqrt(ms + EPS) * g_ref[...]).astype(BF16)
    x2 = x1
    for c in range(hidden // th):
        gate = jnp.dot(h, wi_ref[:, c * th:(c + 1) * th], preferred_element_type=F32)
        up = jnp.dot(h, wi_ref[:, hidden + c * th:hidden + (c + 1) * th], preferred_element_type=F32)
        act = (gate * _sigmoid(gate) * up).astype(BF16)
        x2 = x2 + jnp.dot(act, wd_ref[c * th:(c + 1) * th, :], preferred_element_type=F32)
    if final:
        ms = jnp.mean(x2 * x2, axis=-1, keepdims=True)
        x2 = x2 * lax.rsqrt(ms + EPS) * fg_ref[...]
    o_ref[...] = x2


def _mix_ffn(l, x2, yc, yg, yd, w_out, g, w_ffn_in, w_ffn_out, fg, *, tm, th, final):
    t, d = x2.shape
    tm = min(tm, t)
    kern = functools.partial(_mix_ffn_kernel, th=th, final=final)

    def tok():
        return pl.BlockSpec((tm, d), lambda i: (i, 0))

    def resident(w):
        if w.ndim == 3:
            return pl.BlockSpec((None,) + w.shape[1:], lambda i: (l, 0, 0), pipeline_mode=pl.Buffered(1))
        return pl.BlockSpec(w.shape, lambda i: (0, 0), pipeline_mode=pl.Buffered(1))

    return pl.pallas_call(
        kern,
        grid=(t // tm,),
        in_specs=[tok(), tok(), tok(), tok(), resident(w_out), resident(g),
                  resident(w_ffn_in), resident(w_ffn_out), resident(fg)],
        out_specs=pl.BlockSpec((tm, d), lambda i: (i, 0)),
        out_shape=jax.ShapeDtypeStruct((t, d), F32),
        compiler_params=_cparams(("parallel",)),
        name="mix_ffn",
    )(x2, yc, yg, yd, w_out, g, w_ffn_in, w_ffn_out, fg)


def _regroup_in_proj(w_in, offs):
    lo, hi = offs["g_f"]
    w_main = jnp.concatenate([w_in[..., :lo], w_in[..., hi:]], axis=-1).astype(BF16)
    w_f = jnp.pad(w_in[..., lo:hi], ((0, 0), (0, 0), (0, LANES - (hi - lo)))).astype(BF16)
    return w_main, w_f


def kernel(x, norm_mix_g, w_in, conv_dw, conv_ln_g, conv_ln_b, w_conv_out, gla_wf_up, gla_bf, gla_norm_g,
           diff_lq1, diff_lk1, diff_lq2, diff_lk2, diff_norm_g, w_out, norm_ffn_g, w_ffn_in, w_ffn_out,
           final_norm_g):
    bsz, seq, d = x.shape
    depth = w_in.shape[0]
    rank, hdk = gla_wf_up.shape[1:]
    vd = diff_norm_g.shape[1]
    t = bsz * seq

    names = ("c_val", "c_gate", "g_q", "g_k", "g_v", "g_og", "g_f", "d_q", "d_k", "d_v", "m_conv", "m_gla", "m_diff")
    widths = (d, d, hdk, hdk, d, d, rank, d, d, d, d, d, d)
    offs, start = {}, 0
    for n, wd in zip(names, widths):
        offs[n] = (start, start + wd)
        start += wd
    hp = TILES["diff"]["hp"]
    unit = {"g_q": hdk, "g_k": hdk, "d_q": hp * vd, "d_k": hp * vd, "d_v": hp * vd, "m_diff": hp * vd}
    blk = {}
    for n in names:
        if n != "g_f":
            col = offs[n][0] - (rank if offs[n][0] > offs["g_f"][0] else 0)
            assert col % unit.get(n, d) == 0, n
            blk[n] = col // unit.get(n, d)

    w_main, w_f = _regroup_in_proj(w_in, offs)
    wf_up = jnp.pad(gla_wf_up, ((0, 0), (0, LANES - rank), (0, 0))).astype(BF16)
    w_pw, w_o, w_fi, w_fo = (w.astype(BF16) for w in (w_conv_out, w_out, w_ffn_in, w_ffn_out))
    row = lambda p: p[:, None, :]
    lam_p = jnp.stack([diff_lq1, diff_lk1, diff_lq2, diff_lk2], axis=1).astype(F32)

    x2 = x.reshape(t, d)
    for l in range(depth):
        lambda_init = 0.8 - 0.6 * math.exp(-0.3 * l)
        proj, f_low = _inproj(l, x2, row(norm_mix_g), w_main, w_f, **TILES["inproj"])
        y_conv = _conv(l, proj, conv_dw, row(conv_ln_g), row(conv_ln_b), w_pw, bsz=bsz, seq=seq, blk=blk,
                       **TILES["conv"])
        y_gla = _gla(l, proj, f_low, wf_up, row(gla_bf), row(gla_norm_g), bsz=bsz, seq=seq, blk=blk,
                     **TILES["gla"])
        lam = _diff_lambda(l, lam_p, lambda_init)
        y_diff = _diff(l, lam, proj, row(diff_norm_g), bsz=bsz, seq=seq, lambda_init=lambda_init, blk=blk,
                       **TILES["diff"])
        x2 = _mix_ffn(l, x2, y_conv, y_gla, y_diff, w_o, row(norm_ffn_g), w_fi, w_fo, final_norm_g[None, :],
                      final=(l == depth - 1), **TILES["mix_ffn"])
    return x2.reshape(bsz, seq, d)
```

```python
import functools
import math

import jax
import jax.numpy as jnp
from jax import lax
from jax.experimental import pallas as pl
from jax.experimental.pallas import tpu as pltpu

F32 = jnp.float32
BF16 = jnp.bfloat16

EPS = 1e-6
CHUNK = 64
CONV_KERNEL = 31
CONV_HALO = 32
GLA_HEADS = 4
GLA_RANK = 16
GLA_GATE_NORM = 16.0
DIFF_HEADS = 8
LANES = 128
SUBLANES = 8
VMEM_LIMIT = 56 * 1024 * 1024
LOG2E = 1.4426950408889634
ONES_ROWS = 16

TILES = {
    "inproj": dict(tm=2048, tn=1024),
    "conv": dict(ts=1024, rc=32, rb=128),
    "gla": dict(tg=512),
    "diff": dict(tq=1024, qs=256, hp=2),
    "mix_ffn": dict(tm=512, th=256),
}


def _cparams(sem):
    return pltpu.CompilerParams(dimension_semantics=sem, vmem_limit_bytes=VMEM_LIMIT)


def _sigmoid(x):
    return 1.0 / (1.0 + jnp.exp(-x))


def _log_sigmoid(x):
    return jnp.minimum(x, 0.0) - jnp.log(1.0 + jnp.exp(-jnp.abs(x)))


def _split_bf16(x):
    hi = x.astype(BF16)
    lo = (x - hi.astype(F32)).astype(BF16)
    return hi, lo


def _inproj_kernel(x_ref, g_ref, w_ref, wf_ref, o_ref, f_ref, h_scr):
    @pl.when(pl.program_id(1) == 0)
    def _():
        x = x_ref[...]
        ms = jnp.mean(x * x, axis=-1, keepdims=True)
        h = (x * lax.rsqrt(ms + EPS) * g_ref[...]).astype(BF16)
        h_scr[...] = h
        f_ref[...] = jnp.dot(h, wf_ref[...], preferred_element_type=F32)

    o_ref[...] = jnp.dot(h_scr[...], w_ref[...], preferred_element_type=F32).astype(BF16)


def _layer(l, *block):
    return pl.BlockSpec((None,) + block, lambda *_: (l,) + (0,) * len(block))


def _inproj(l, x2, g, w_main, w_f, *, tm, tn):
    t, d = x2.shape
    tm = min(tm, t)
    nw = w_main.shape[2]
    return pl.pallas_call(
        _inproj_kernel,
        grid=(t // tm, nw // tn),
        in_specs=[
            pl.BlockSpec((tm, d), lambda i, j: (i, 0)),
            _layer(l, 1, d),
            pl.BlockSpec((None, d, tn), lambda i, j: (l, 0, j)),
            _layer(l, d, LANES),
        ],
        out_specs=[
            pl.BlockSpec((tm, tn), lambda i, j: (i, j)),
            pl.BlockSpec((tm, LANES), lambda i, j: (i, 0)),
        ],
        out_shape=[
            jax.ShapeDtypeStruct((t, nw), BF16),
            jax.ShapeDtypeStruct((t, LANES), F32),
        ],
        scratch_shapes=[pltpu.VMEM((tm, d), BF16)],
        compiler_params=_cparams(("parallel", "arbitrary")),
        name="inproj",
    )(x2, g, w_main, w_f)


def _conv_kernel(a_ref, gt_ref, m_ref, wdw_ref, lng_ref, lnb_ref, wpw_ref, o_ref, u_scr, c_scr, v_scr,
                 *, ts, rc, rb):
    d = a_ref.shape[1]

    @pl.when(pl.program_id(1) == 0)
    def _():
        u_scr[0:CONV_HALO, :] = jnp.zeros((CONV_HALO, d), F32)

    def glu_body(r, carry):
        r0 = pl.multiple_of(r * rc, rc)
        a = a_ref[pl.ds(r0, rc), :].astype(F32)
        gt = gt_ref[pl.ds(r0, rc), :].astype(F32)
        u_scr[pl.ds(CONV_HALO + r0, rc), :] = a * _sigmoid(gt)
        return carry

    lax.fori_loop(0, ts // rc, glu_body, 0)

    first = CONV_HALO - (CONV_KERNEL - 1)

    def conv_body(i, carry):
        t0 = pl.multiple_of(i * rb, rb)
        for s in range(d // LANES):
            lanes = slice(s * LANES, (s + 1) * LANES)
            acc = None
            for r in range(SUBLANES):
                rows = rb + (SUBLANES if r else 0)
                part = None
                for j in range(first, first + CONV_KERNEL):
                    if j % SUBLANES != r:
                        continue
                    term = (u_scr[pl.ds(t0 + j - r, rows), lanes] * wdw_ref[j - first:j - first + 1, lanes])
                    part = term if part is None else part + term
                if part is None:
                    continue
                if r:
                    part = pltpu.roll(part, rows - r, 0)[0:rb]
                acc = part if acc is None else acc + part
            c_scr[pl.ds(t0, rb), lanes] = acc
        return carry

    lax.fori_loop(0, ts // rb, conv_body, 0)
    u_scr[0:CONV_HALO, :] = u_scr[ts:ts + CONV_HALO, :]

    lng = lng_ref[...]
    lnb = lnb_ref[...]

    def ln_body(r, carry):
        r0 = pl.multiple_of(r * rc, rc)
        x = c_scr[pl.ds(r0, rc), :]
        mu = jnp.mean(x, axis=-1, keepdims=True)
        xc = x - mu
        var = jnp.mean(xc * xc, axis=-1, keepdims=True)
        y = xc * lax.rsqrt(var + EPS) * lng + lnb
        v_scr[pl.ds(r0, rc), :] = (y * _sigmoid(y)).astype(BF16)
        return carry

    lax.fori_loop(0, ts // rc, ln_body, 0, unroll=4)

    o = jnp.dot(v_scr[...], wpw_ref[...], preferred_element_type=F32)
    o_ref[...] = (o * _sigmoid(m_ref[...].astype(F32))).astype(BF16)


def _conv(l, proj, wdw, lng, lnb, wpw, *, bsz, seq, ts, rc, rb, blk):
    d = wpw.shape[1]
    ts = min(ts, seq)
    nst = seq // ts
    kern = functools.partial(_conv_kernel, ts=ts, rc=rc, rb=min(rb, ts))

    def col(c):
        return pl.BlockSpec((ts, d), lambda b, s: (b * nst + s, c))

    return pl.pallas_call(
        kern,
        grid=(bsz, nst),
        in_specs=[col(blk["c_val"]), col(blk["c_gate"]), col(blk["m_conv"]),
                  _layer(l, *wdw.shape[1:]), _layer(l, 1, d), _layer(l, 1, d), _layer(l, d, d)],
        out_specs=pl.BlockSpec((ts, d), lambda b, s: (b * nst + s, 0)),
        out_shape=jax.ShapeDtypeStruct((bsz * seq, d), BF16),
        scratch_shapes=[pltpu.VMEM((CONV_HALO + ts, d), F32), pltpu.VMEM((ts, d), F32), pltpu.VMEM((ts, d), BF16)],
        compiler_params=_cparams(("parallel", "arbitrary")),
        name="conv",
    )(proj, proj, proj, wdw, lng, lnb, wpw)


def _gla_kernel(q_ref, k_ref, v_ref, og_ref, m_ref, f_ref, wf_ref, bf_ref, ng_ref, o_ref,
                st_scr, qg_scr, kg_scr, kd_scr, causal_scr, cumsum_scr, g_scr, o_scr, *, tg, dk, dv):
    nc = tg // CHUNK

    @pl.when(pl.program_id(1) == 0)
    def _():
        st_scr[...] = jnp.zeros(st_scr.shape, F32)
        row = lax.broadcasted_iota(jnp.int32, (tg, tg), 0)
        col = lax.broadcasted_iota(jnp.int32, (tg, tg), 1)
        same = jnp.where(row // CHUNK == col // CHUNK, 1.0, 0.0)
        causal = jnp.where(row >= col, same, 0.0)
        causal_scr[...] = causal
        cumsum_scr[...] = causal.astype(BF16)

    chunk_cumsum = cumsum_scr[...]

    f_hi, f_lo = _split_bf16(f_ref[...])
    wf = wf_ref[...]
    logit = (jnp.dot(f_hi, wf, preferred_element_type=F32)
             + jnp.dot(f_lo, wf, preferred_element_type=F32) + bf_ref[...])
    logf = _log_sigmoid(logit) * (1.0 / GLA_GATE_NORM)
    l_hi, l_lo = _split_bf16(logf)
    g_scr[...] = (jnp.dot(chunk_cumsum, l_hi, preferred_element_type=F32)
                  + jnp.dot(chunk_cumsum, l_lo, preferred_element_type=F32))
    last = [g_scr[(c + 1) * CHUNK - 1:(c + 1) * CHUNK, :] for c in range(nc)]
    qg_scr[...] = (q_ref[...].astype(F32) * (dk ** -0.5) * jnp.exp(g_scr[...])).astype(BF16)
    kg_scr[...] = (k_ref[...].astype(F32) * jnp.exp(-g_scr[...])).astype(BF16)
    for c in range(nc):
        rows = slice(c * CHUNK, (c + 1) * CHUNK)
        kd_scr[rows, :] = (k_ref[rows, :].astype(F32) * jnp.exp(last[c] - g_scr[rows, :])).astype(BF16)
    decay = [jnp.exp(r) for r in last]
    ng = ng_ref[...]

    for h in range(GLA_HEADS):
        ks = slice(h * dk, (h + 1) * dk)
        vs = slice(h * dv, (h + 1) * dv)
        att = lax.dot_general(qg_scr[:, ks], kg_scr[:, ks], (((1,), (1,)), ((), ())),
                              preferred_element_type=F32)
        att = jnp.where(causal_scr[...] > 0.0, att, 0.0).astype(BF16)
        o_scr[...] = jnp.dot(att, v_ref[:, vs], preferred_element_type=F32)
        st = st_scr[h]
        for c in range(nc):
            rows = slice(c * CHUNK, (c + 1) * CHUNK)
            o_scr[rows, :] += lax.dot_general(qg_scr[rows, ks], st.astype(BF16), (((1,), (1,)), ((), ())),
                                              preferred_element_type=F32)
            st = st * decay[c][:, ks] + lax.dot_general(
                v_ref[rows, vs], kd_scr[rows, ks], (((0,), (0,)), ((), ())), preferred_element_type=F32)
        st_scr[h] = st
        o = o_scr[...]
        o = o * lax.rsqrt(jnp.mean(o * o, axis=-1, keepdims=True) + EPS) * ng
        og = og_ref[:, vs].astype(F32)
        gate = _sigmoid(m_ref[:, vs].astype(F32))
        o_ref[:, vs] = (gate * (o * (og * _sigmoid(og)))).astype(BF16)


def _gla(l, proj, f_low, wf_up, bf, ng, *, bsz, seq, tg, blk):
    hdk = wf_up.shape[2]
    dk = hdk // GLA_HEADS
    dv = ng.shape[2]
    hdv = GLA_HEADS * dv
    tg = min(tg, seq)
    nst = seq // tg
    kern = functools.partial(_gla_kernel, tg=tg, dk=dk, dv=dv)

    def col(width, c):
        return pl.BlockSpec((tg, width), lambda b, s: (b * nst + s, c))

    return pl.pallas_call(
        kern,
        grid=(bsz, nst),
        in_specs=[col(hdk, blk["g_q"]), col(hdk, blk["g_k"]), col(hdv, blk["g_v"]), col(hdv, blk["g_og"]),
                  col(hdv, blk["m_gla"]), col(LANES, 0),
                  _layer(l, LANES, hdk), _layer(l, 1, hdk), _layer(l, 1, dv)],
        out_specs=pl.BlockSpec((tg, hdv), lambda b, s: (b * nst + s, 0)),
        out_shape=jax.ShapeDtypeStruct((bsz * seq, hdv), BF16),
        scratch_shapes=([pltpu.VMEM((GLA_HEADS, dv, dk), F32)] + [pltpu.VMEM((tg, hdk), BF16)] * 3
                        + [pltpu.VMEM((tg, tg), F32), pltpu.VMEM((tg, tg), BF16),
                           pltpu.VMEM((tg, hdk), F32), pltpu.VMEM((tg, dv), F32)]),
        compiler_params=_cparams(("parallel", "arbitrary")),
        name="gla",
    )(proj, proj, proj, proj, proj, f_low, wf_up, bf, ng)


def _diff_kernel(lam_ref, q_ref, k_ref, v_ref, m_ref, ng_ref, o_ref, vt_scr, qq_scr, m_scr, acc_scr, s_scr,
                 *, tq, qs, hp, hd, lambda_init):
    qi = pl.program_id(2)
    seq = k_ref.shape[0]
    vd = 2 * hd
    heads = [slice(h * vd, (h + 1) * vd) for h in range(hp)]
    strips = [(h, c) for h in range(hp) for c in range(2 * tq // qs)]

    @pl.when(qi == 0)
    def _():
        for h in range(hp):
            for c in range(seq // tq):
                rows = slice(c * tq, (c + 1) * tq)
                vt_scr[h, 0:vd, rows] = v_ref[rows, heads[h]].astype(F32).T.astype(BF16)
            vt_scr[h, vd:vd + ONES_ROWS, :] = jnp.ones((ONES_ROWS, seq), BF16)

    for h in range(hp):
        q = q_ref[:, heads[h]].astype(F32) * (hd ** -0.5 * LOG2E)
        lane = lax.broadcasted_iota(jnp.int32, q.shape, 1)
        qq_scr[h, 0:tq, :] = jnp.where(lane < hd, q, 0.0).astype(BF16)
        qq_scr[h, tq:2 * tq, :] = jnp.where(lane >= hd, q, 0.0).astype(BF16)

    m_scr[...] = jnp.full(m_scr.shape, -jnp.inf, F32)
    acc_scr[...] = jnp.zeros(acc_scr.shape, F32)

    def step(j, masked):
        c0 = pl.multiple_of(j * tq, tq)

        def keys(c):
            q0 = (c * qs) % tq
            return q0, (min(tq, q0 + qs) if masked else tq)

        for i, (h, c) in enumerate(strips):
            q0, nk = keys(c)
            st = lax.dot_general(k_ref[pl.ds(c0, nk), heads[h]], qq_scr[h, c * qs:(c + 1) * qs, :],
                                 (((1,), (1,)), ((), ())), preferred_element_type=F32)
            if masked:
                kc = lax.broadcasted_iota(jnp.int32, st.shape, 0) // CHUNK
                qc = (lax.broadcasted_iota(jnp.int32, st.shape, 1) + q0) // CHUNK
                st = jnp.where(kc <= qc, st, -jnp.inf)
            s_scr[i, 0:nk, :] = st
        for i, (h, c) in enumerate(strips):
            cols = slice(c * qs, (c + 1) * qs)
            q0, nk = keys(c)
            m_prev = m_scr[h, :, cols]
            m_new = jnp.maximum(m_prev, jnp.max(s_scr[i, 0:nk, :], axis=0, keepdims=True))
            alpha = jnp.exp2(m_prev - m_new)
            pt = jnp.exp2(s_scr[i, 0:nk, :] - m_new)
            acc_scr[h, :, cols] = alpha * acc_scr[h, :, cols] + jnp.dot(
                vt_scr[h, :, pl.ds(c0, nk)], pt.astype(BF16), preferred_element_type=F32)
            m_scr[h, :, cols] = m_new

    def body(j, carry):
        step(j, False)
        return carry

    lax.fori_loop(0, qi, body, 0)
    step(qi, True)

    lam = lam_ref[0]
    for h in range(hp):
        ot = acc_scr[h, 0:vd, :] / acc_scr[h, vd:vd + 1, :]
        od = (ot[:, 0:tq] - lam * ot[:, tq:2 * tq]).T
        od = od * lax.rsqrt(jnp.mean(od * od, axis=-1, keepdims=True) + EPS) * ng_ref[...] * (1.0 - lambda_init)
        o_ref[:, heads[h]] = (od * _sigmoid(m_ref[:, heads[h]].astype(F32))).astype(BF16)


def _diff(l, lam, proj, ng, *, bsz, seq, tq, qs, hp, lambda_init, blk):
    vd = ng.shape[2]
    hd = vd // 2
    tq = min(tq, seq)
    nq = seq // tq
    w = hp * vd
    kern = functools.partial(_diff_kernel, tq=tq, qs=qs, hp=hp, hd=hd, lambda_init=lambda_init)
    return pl.pallas_call(
        kern,
        grid=(bsz, DIFF_HEADS // hp, nq),
        in_specs=[
            pl.BlockSpec(memory_space=pltpu.SMEM),
            pl.BlockSpec((tq, w), lambda b, h, i: (b * nq + i, blk["d_q"] + h)),
            pl.BlockSpec((seq, w), lambda b, h, i: (b, blk["d_k"] + h)),
            pl.BlockSpec((seq, w), lambda b, h, i: (b, blk["d_v"] + h)),
            pl.BlockSpec((tq, w), lambda b, h, i: (b * nq + i, blk["m_diff"] + h)),
            _layer(l, 1, vd),
        ],
        out_specs=pl.BlockSpec((tq, w), lambda b, h, i: (b * nq + i, h)),
        out_shape=jax.ShapeDtypeStruct((bsz * seq, DIFF_HEADS * vd), BF16),
        scratch_shapes=[pltpu.VMEM((hp, vd + ONES_ROWS, seq), BF16), pltpu.VMEM((hp, 2 * tq, vd), BF16),
                        pltpu.VMEM((hp, 1, 2 * tq), F32), pltpu.VMEM((hp, vd + ONES_ROWS, 2 * tq), F32),
                        pltpu.VMEM((hp * 2 * tq // qs, tq, qs), F32)],
        compiler_params=_cparams(("parallel", "parallel", "arbitrary")),
        name="diff",
    )(lam, proj, proj, proj, proj, ng)


def _lambda_kernel(p_ref, o_ref, *, lambda_init):
    p = p_ref[...]
    s1 = jnp.sum(p[0:1, :] * p[1:2, :], axis=-1, keepdims=True)
    s2 = jnp.sum(p[2:3, :] * p[3:4, :], axis=-1, keepdims=True)
    o_ref[...] = jnp.exp(s1) - jnp.exp(s2) + lambda_init


def _diff_lambda(l, p, lambda_init):
    out = pl.pallas_call(
        functools.partial(_lambda_kernel, lambda_init=lambda_init),
        grid=(1,),
        in_specs=[_layer(l, *p.shape[1:])],
        out_specs=pl.BlockSpec((1, 1), lambda i: (0, 0)),
        out_shape=jax.ShapeDtypeStruct((1, 1), F32),
        name="diff_lambda",
    )(p)
    return out.reshape(1)


def _mix_ffn_kernel(x_ref, yc_ref, yg_ref, yd_ref, wo_ref, g_ref, wi_ref, wd_ref, fg_ref, o_ref, *, th, final):
    hidden = wd_ref.shape[0]
    y = (yc_ref[...].astype(F32) + yg_ref[...].astype(F32) + yd_ref[...].astype(F32)).astype(BF16)
    x1 = x_ref[...] + jnp.dot(y, wo_ref[...], preferred_element_type=F32)
    ms = jnp.mean(x1 * x1, axis=-1, keepdims=True)
    h = (x1 * lax.rsqrt(ms + EPS) * g_ref[...]).astype(BF16)
    x2 = x1
    for c in range(hidden // th):
        gate = jnp.dot(h, wi_ref[:, c * th:(c + 1) * th], preferred_element_type=F32)
        up = jnp.dot(h, wi_ref[:, hidden + c * th:hidden + (c + 1) * th], preferred_element_type=F32)
        act = (gate * _sigmoid(gate) * up).astype(BF16)
        x2 = x2 + jnp.dot(act, wd_ref[c * th:(c + 1) * th, :], preferred_element_type=F32)
    if final:
        ms = jnp.mean(x2 * x2, axis=-1, keepdims=True)
        x2 = x2 * lax.rsqrt(ms + EPS) * fg_ref[...]
    o_ref[...] = x2


def _mix_ffn(l, x2, yc, yg, yd, w_out, g, w_ffn_in, w_ffn_out, fg, *, tm, th, final):
    t, d = x2.shape
    tm = min(tm, t)
    kern = functools.partial(_mix_ffn_kernel, th=th, final=final)

    def tok():
        return pl.BlockSpec((tm, d), lambda i: (i, 0))

    def resident(w):
        if w.ndim == 3:
            return pl.BlockSpec((None,) + w.shape[1:], lambda i: (l, 0, 0), pipeline_mode=pl.Buffered(1))
        return pl.BlockSpec(w.shape, lambda i: (0, 0), pipeline_mode=pl.Buffered(1))

    return pl.pallas_call(
        kern,
        grid=(t // tm,),
        in_specs=[tok(), tok(), tok(), tok(), resident(w_out), resident(g),
                  resident(w_ffn_in), resident(w_ffn_out), resident(fg)],
        out_specs=pl.BlockSpec((tm, d), lambda i: (i, 0)),
        out_shape=jax.ShapeDtypeStruct((t, d), F32),
        compiler_params=_cparams(("parallel",)),
        name="mix_ffn",
    )(x2, yc, yg, yd, w_out, g, w_ffn_in, w_ffn_out, fg)


def _regroup_in_proj(w_in, offs):
    lo, hi = offs["g_f"]
    w_in = w_in.astype(BF16)
    w_main = jnp.concatenate([w_in[..., :lo], w_in[..., hi:]], axis=-1)
    w_f = jnp.pad(w_in[..., lo:hi], ((0, 0), (0, 0), (0, LANES - (hi - lo))))
    return w_main, w_f


def kernel(x, norm_mix_g, w_in, conv_dw, conv_ln_g, conv_ln_b, w_conv_out, gla_wf_up, gla_bf, gla_norm_g,
           diff_lq1, diff_lk1, diff_lq2, diff_lk2, diff_norm_g, w_out, norm_ffn_g, w_ffn_in, w_ffn_out,
           final_norm_g):
    bsz, seq, d = x.shape
    depth = w_in.shape[0]
    rank, hdk = gla_wf_up.shape[1:]
    vd = diff_norm_g.shape[1]
    t = bsz * seq

    names = ("c_val", "c_gate", "g_q", "g_k", "g_v", "g_og", "g_f", "d_q", "d_k", "d_v", "m_conv", "m_gla", "m_diff")
    widths = (d, d, hdk, hdk, d, d, rank, d, d, d, d, d, d)
    offs, start = {}, 0
    for n, wd in zip(names, widths):
        offs[n] = (start, start + wd)
        start += wd
    hp = TILES["diff"]["hp"]
    unit = {"g_q": hdk, "g_k": hdk, "d_q": hp * vd, "d_k": hp * vd, "d_v": hp * vd, "m_diff": hp * vd}
    blk = {}
    for n in names:
        if n != "g_f":
            col = offs[n][0] - (rank if offs[n][0] > offs["g_f"][0] else 0)
            assert col % unit.get(n, d) == 0, n
            blk[n] = col // unit.get(n, d)

    w_main, w_f = _regroup_in_proj(w_in, offs)
    wf_up = jnp.pad(gla_wf_up, ((0, 0), (0, LANES - rank), (0, 0))).astype(BF16)
    w_pw, w_o, w_fi, w_fo = (w.astype(BF16) for w in (w_conv_out, w_out, w_ffn_in, w_ffn_out))
    row = lambda p: p[:, None, :]
    lam_p = jnp.stack([diff_lq1, diff_lk1, diff_lq2, diff_lk2], axis=1).astype(F32)

    x2 = x.reshape(t, d)
    for l in range(depth):
        lambda_init = 0.8 - 0.6 * math.exp(-0.3 * l)
        proj, f_low = _inproj(l, x2, row(norm_mix_g), w_main, w_f, **TILES["inproj"])
        y_conv = _conv(l, proj, conv_dw, row(conv_ln_g), row(conv_ln_b), w_pw, bsz=bsz, seq=seq, blk=blk,
                       **TILES["conv"])
        y_gla = _gla(l, proj, f_low, wf_up, row(gla_bf), row(gla_norm_g), bsz=bsz, seq=seq, blk=blk,
                     **TILES["gla"])
        lam = _diff_lambda(l, lam_p, lambda_init)
        y_diff = _diff(l, lam, proj, row(diff_norm_g), bsz=bsz, seq=seq, lambda_init=lambda_init, blk=blk,
                       **TILES["diff"])
        x2 = _mix_ffn(l, x2, y_conv, y_gla, y_diff, w_o, row(norm_ffn_g), w_fi, w_fo, final_norm_g[None, :],
                      final=(l == depth - 1), **TILES["mix_ffn"])
    return x2.reshape(bsz, seq, d)
```

```python
import functools
import math

import jax
import jax.numpy as jnp
from jax import lax
from jax.experimental import pallas as pl
from jax.experimental.pallas import tpu as pltpu

F32 = jnp.float32
BF16 = jnp.bfloat16

EPS = 1e-6
CHUNK = 64
CONV_KERNEL = 31
CONV_HALO = 32
GLA_HEADS = 4
GLA_RANK = 16
GLA_GATE_NORM = 16.0
DIFF_HEADS = 8
LANES = 128
SUBLANES = 8
VMEM_LIMIT = 56 * 1024 * 1024
LOG2E = 1.4426950408889634
ONES_ROWS = 16

TILES = {
    "inproj": dict(tm=256, tn=1024),
    "conv": dict(ts=512, rc=32, rb=128),
    "gla": dict(tg=512),
    "diff": dict(tq=1024, qs=256, hp=2),
    "mix_ffn": dict(tm=512, th=256),
}


def _cparams(sem):
    return pltpu.CompilerParams(dimension_semantics=sem, vmem_limit_bytes=VMEM_LIMIT)


def _sigmoid(x):
    return 1.0 / (1.0 + jnp.exp(-x))


def _log_sigmoid(x):
    return jnp.minimum(x, 0.0) - jnp.log(1.0 + jnp.exp(-jnp.abs(x)))


def _split_bf16(x):
    hi = x.astype(BF16)
    lo = (x - hi.astype(F32)).astype(BF16)
    return hi, lo


def _inproj_kernel(x_ref, g_ref, w_ref, wf_ref, o_ref, f_ref, *, tn):
    x = x_ref[...]
    ms = jnp.mean(x * x, axis=-1, keepdims=True)
    h = (x * lax.rsqrt(ms + EPS) * g_ref[...]).astype(BF16)
    f_ref[...] = jnp.dot(h, wf_ref[...], preferred_element_type=F32)
    for j in range(w_ref.shape[1] // tn):
        cols = slice(j * tn, (j + 1) * tn)
        o_ref[:, cols] = jnp.dot(h, w_ref[:, cols], preferred_element_type=F32).astype(BF16)


def _layer(l, *block):
    return pl.BlockSpec((None,) + block, lambda *_: (l,) + (0,) * len(block))


def _inproj(l, x2, g, w_main, w_f, *, tm, tn):
    t, d = x2.shape
    tm = min(tm, t)
    nw = w_main.shape[2]
    def resident(*block):
        return pl.BlockSpec((None,) + block, lambda i: (l, 0, 0), pipeline_mode=pl.Buffered(1))

    return pl.pallas_call(
        functools.partial(_inproj_kernel, tn=tn),
        grid=(t // tm,),
        in_specs=[pl.BlockSpec((tm, d), lambda i: (i, 0)), resident(1, d), resident(d, nw), resident(d, LANES)],
        out_specs=[pl.BlockSpec((tm, nw), lambda i: (i, 0)), pl.BlockSpec((tm, LANES), lambda i: (i, 0))],
        out_shape=[jax.ShapeDtypeStruct((t, nw), BF16), jax.ShapeDtypeStruct((t, LANES), F32)],
        compiler_params=_cparams(("parallel",)),
        name="inproj",
    )(x2, g, w_main, w_f)


def _conv_kernel(a_ref, gt_ref, m_ref, wdw_ref, lng_ref, lnb_ref, wpw_ref, o_ref, u_scr, c_scr, v_scr,
                 *, ts, rc, rb):
    d = a_ref.shape[1]

    @pl.when(pl.program_id(1) == 0)
    def _():
        u_scr[0:CONV_HALO, :] = jnp.zeros((CONV_HALO, d), F32)

    def glu_body(r, carry):
        r0 = pl.multiple_of(r * rc, rc)
        a = a_ref[pl.ds(r0, rc), :].astype(F32)
        gt = gt_ref[pl.ds(r0, rc), :].astype(F32)
        u_scr[pl.ds(CONV_HALO + r0, rc), :] = a * _sigmoid(gt)
        return carry

    lax.fori_loop(0, ts // rc, glu_body, 0)

    first = CONV_HALO - (CONV_KERNEL - 1)

    def conv_body(i, carry):
        t0 = pl.multiple_of(i * rb, rb)
        for s in range(d // LANES):
            lanes = slice(s * LANES, (s + 1) * LANES)
            acc = None
            for r in range(SUBLANES):
                rows = rb + (SUBLANES if r else 0)
                part = None
                for j in range(first, first + CONV_KERNEL):
                    if j % SUBLANES != r:
                        continue
                    term = (u_scr[pl.ds(t0 + j - r, rows), lanes] * wdw_ref[j - first:j - first + 1, lanes])
                    part = term if part is None else part + term
                if part is None:
                    continue
                if r:
                    part = pltpu.roll(part, rows - r, 0)[0:rb]
                acc = part if acc is None else acc + part
            c_scr[pl.ds(t0, rb), lanes] = acc
        return carry

    lax.fori_loop(0, ts // rb, conv_body, 0)
    u_scr[0:CONV_HALO, :] = u_scr[ts:ts + CONV_HALO, :]

    lng = lng_ref[...]
    lnb = lnb_ref[...]

    def ln_body(r, carry):
        r0 = pl.multiple_of(r * rc, rc)
        x = c_scr[pl.ds(r0, rc), :]
        mu = jnp.mean(x, axis=-1, keepdims=True)
        xc = x - mu
        var = jnp.mean(xc * xc, axis=-1, keepdims=True)
        y = xc * lax.rsqrt(var + EPS) * lng + lnb
        v_scr[pl.ds(r0, rc), :] = (y * _sigmoid(y)).astype(BF16)
        return carry

    lax.fori_loop(0, ts // rc, ln_body, 0, unroll=4)

    o = jnp.dot(v_scr[...], wpw_ref[...], preferred_element_type=F32)
    o_ref[...] = (o * _sigmoid(m_ref[...].astype(F32))).astype(BF16)


def _conv(l, proj, wdw, lng, lnb, wpw, *, bsz, seq, ts, rc, rb, blk):
    d = wpw.shape[1]
    ts = min(ts, seq)
    nst = seq // ts
    kern = functools.partial(_conv_kernel, ts=ts, rc=rc, rb=min(rb, ts))

    def col(c):
        return pl.BlockSpec((ts, d), lambda b, s: (b * nst + s, c))

    return pl.pallas_call(
        kern,
        grid=(bsz, nst),
        in_specs=[col(blk["c_val"]), col(blk["c_gate"]), col(blk["m_conv"]),
                  _layer(l, *wdw.shape[1:]), _layer(l, 1, d), _layer(l, 1, d), _layer(l, d, d)],
        out_specs=pl.BlockSpec((ts, d), lambda b, s: (b * nst + s, 0)),
        out_shape=jax.ShapeDtypeStruct((bsz * seq, d), BF16),
        scratch_shapes=[pltpu.VMEM((CONV_HALO + ts, d), F32), pltpu.VMEM((ts, d), F32), pltpu.VMEM((ts, d), BF16)],
        compiler_params=_cparams(("parallel", "arbitrary")),
        name="conv",
    )(proj, proj, proj, wdw, lng, lnb, wpw)


def _gla_kernel(q_ref, k_ref, v_ref, og_ref, m_ref, f_ref, wf_ref, bf_ref, ng_ref, o_ref,
                st_scr, qg_scr, kg_scr, kd_scr, causal_scr, cumsum_scr, g_scr, o_scr, *, tg, dk, dv):
    nc = tg // CHUNK

    @pl.when(pl.program_id(1) == 0)
    def _():
        st_scr[...] = jnp.zeros(st_scr.shape, F32)
        row = lax.broadcasted_iota(jnp.int32, (tg, tg), 0)
        col = lax.broadcasted_iota(jnp.int32, (tg, tg), 1)
        same = jnp.where(row // CHUNK == col // CHUNK, 1.0, 0.0)
        causal = jnp.where(row >= col, same, 0.0)
        causal_scr[...] = causal
        cumsum_scr[...] = causal.astype(BF16)

    chunk_cumsum = cumsum_scr[...]

    f_hi, f_lo = _split_bf16(f_ref[...])
    wf = wf_ref[...]
    logit = (jnp.dot(f_hi, wf, preferred_element_type=F32)
             + jnp.dot(f_lo, wf, preferred_element_type=F32) + bf_ref[...])
    logf = _log_sigmoid(logit) * (1.0 / GLA_GATE_NORM)
    l_hi, l_lo = _split_bf16(logf)
    g_scr[...] = (jnp.dot(chunk_cumsum, l_hi, preferred_element_type=F32)
                  + jnp.dot(chunk_cumsum, l_lo, preferred_element_type=F32))
    last = [g_scr[(c + 1) * CHUNK - 1:(c + 1) * CHUNK, :] for c in range(nc)]
    qg_scr[...] = (q_ref[...].astype(F32) * (dk ** -0.5) * jnp.exp(g_scr[...])).astype(BF16)
    kg_scr[...] = (k_ref[...].astype(F32) * jnp.exp(-g_scr[...])).astype(BF16)
    for c in range(nc):
        rows = slice(c * CHUNK, (c + 1) * CHUNK)
        kd_scr[rows, :] = (k_ref[rows, :].astype(F32) * jnp.exp(last[c] - g_scr[rows, :])).astype(BF16)
    decay = [jnp.exp(r) for r in last]
    ng = ng_ref[...]

    for h in range(GLA_HEADS):
        ks = slice(h * dk, (h + 1) * dk)
        vs = slice(h * dv, (h + 1) * dv)
        att = lax.dot_general(qg_scr[:, ks], kg_scr[:, ks], (((1,), (1,)), ((), ())),
                              preferred_element_type=F32)
        att = jnp.where(causal_scr[...] > 0.0, att, 0.0).astype(BF16)
        o_scr[...] = jnp.dot(att, v_ref[:, vs], preferred_element_type=F32)
        st = st_scr[h]
        for c in range(nc):
            rows = slice(c * CHUNK, (c + 1) * CHUNK)
            o_scr[rows, :] += lax.dot_general(qg_scr[rows, ks], st.astype(BF16), (((1,), (1,)), ((), ())),
                                              preferred_element_type=F32)
            st = st * decay[c][:, ks] + lax.dot_general(
                v_ref[rows, vs], kd_scr[rows, ks], (((0,), (0,)), ((), ())), preferred_element_type=F32)
        st_scr[h] = st
        o = o_scr[...]
        o = o * lax.rsqrt(jnp.mean(o * o, axis=-1, keepdims=True) + EPS) * ng
        og = og_ref[:, vs].astype(F32)
        gate = _sigmoid(m_ref[:, vs].astype(F32))
        o_ref[:, vs] = (gate * (o * (og * _sigmoid(og)))).astype(BF16)


def _gla(l, proj, f_low, wf_up, bf, ng, *, bsz, seq, tg, blk):
    hdk = wf_up.shape[2]
    dk = hdk // GLA_HEADS
    dv = ng.shape[2]
    hdv = GLA_HEADS * dv
    tg = min(tg, seq)
    nst = seq // tg
    kern = functools.partial(_gla_kernel, tg=tg, dk=dk, dv=dv)

    def col(width, c):
        return pl.BlockSpec((tg, width), lambda b, s: (b * nst + s, c))

    return pl.pallas_call(
        kern,
        grid=(bsz, nst),
        in_specs=[col(hdk, blk["g_q"]), col(hdk, blk["g_k"]), col(hdv, blk["g_v"]), col(hdv, blk["g_og"]),
                  col(hdv, blk["m_gla"]), col(LANES, 0),
                  _layer(l, LANES, hdk), _layer(l, 1, hdk), _layer(l, 1, dv)],
        out_specs=pl.BlockSpec((tg, hdv), lambda b, s: (b * nst + s, 0)),
        out_shape=jax.ShapeDtypeStruct((bsz * seq, hdv), BF16),
        scratch_shapes=([pltpu.VMEM((GLA_HEADS, dv, dk), F32)] + [pltpu.VMEM((tg, hdk), BF16)] * 3
                        + [pltpu.VMEM((tg, tg), F32), pltpu.VMEM((tg, tg), BF16),
                           pltpu.VMEM((tg, hdk), F32), pltpu.VMEM((tg, dv), F32)]),
        compiler_params=_cparams(("parallel", "arbitrary")),
        name="gla",
    )(proj, proj, proj, proj, proj, f_low, wf_up, bf, ng)


def _diff_kernel(lam_ref, q_ref, k_ref, v_ref, m_ref, ng_ref, o_ref, vt_scr, qq_scr, m_scr, acc_scr, s_scr,
                 *, tq, qs, hp, hd, lambda_init):
    qi = pl.program_id(2)
    seq = k_ref.shape[0]
    vd = 2 * hd
    heads = [slice(h * vd, (h + 1) * vd) for h in range(hp)]
    strips = [(h, c) for h in range(hp) for c in range(2 * tq // qs)]

    @pl.when(qi == 0)
    def _():
        for h in range(hp):
            for c in range(seq // tq):
                rows = slice(c * tq, (c + 1) * tq)
                vt_scr[h, 0:vd, rows] = v_ref[rows, heads[h]].astype(F32).T.astype(BF16)
            vt_scr[h, vd:vd + ONES_ROWS, :] = jnp.ones((ONES_ROWS, seq), BF16)

    for h in range(hp):
        q = q_ref[:, heads[h]].astype(F32) * (hd ** -0.5 * LOG2E)
        lane = lax.broadcasted_iota(jnp.int32, q.shape, 1)
        qq_scr[h, 0:tq, :] = jnp.where(lane < hd, q, 0.0).astype(BF16)
        qq_scr[h, tq:2 * tq, :] = jnp.where(lane >= hd, q, 0.0).astype(BF16)

    m_scr[...] = jnp.full(m_scr.shape, -jnp.inf, F32)
    acc_scr[...] = jnp.zeros(acc_scr.shape, F32)

    def step(j, masked):
        c0 = pl.multiple_of(j * tq, tq)

        def keys(c):
            q0 = (c * qs) % tq
            return q0, (min(tq, q0 + qs) if masked else tq)

        for i, (h, c) in enumerate(strips):
            q0, nk = keys(c)
            st = lax.dot_general(k_ref[pl.ds(c0, nk), heads[h]], qq_scr[h, c * qs:(c + 1) * qs, :],
                                 (((1,), (1,)), ((), ())), preferred_element_type=F32)
            if masked:
                kc = lax.broadcasted_iota(jnp.int32, st.shape, 0) // CHUNK
                qc = (lax.broadcasted_iota(jnp.int32, st.shape, 1) + q0) // CHUNK
                st = jnp.where(kc <= qc, st, -jnp.inf)
            s_scr[i, 0:nk, :] = st
        for i, (h, c) in enumerate(strips):
            cols = slice(c * qs, (c + 1) * qs)
            q0, nk = keys(c)
            m_prev = m_scr[h, :, cols]
            m_new = jnp.maximum(m_prev, jnp.max(s_scr[i, 0:nk, :], axis=0, keepdims=True))
            alpha = jnp.exp2(m_prev - m_new)
            pt = jnp.exp2(s_scr[i, 0:nk, :] - m_new)
            acc_scr[h, :, cols] = alpha * acc_scr[h, :, cols] + jnp.dot(
                vt_scr[h, :, pl.ds(c0, nk)], pt.astype(BF16), preferred_element_type=F32)
            m_scr[h, :, cols] = m_new

    def body(j, carry):
        step(j, False)
        return carry

    lax.fori_loop(0, qi, body, 0)
    step(qi, True)

    lam = lam_ref[0]
    for h in range(hp):
        ot = acc_scr[h, 0:vd, :] / acc_scr[h, vd:vd + 1, :]
        od = (ot[:, 0:tq] - lam * ot[:, tq:2 * tq]).T
        od = od * lax.rsqrt(jnp.mean(od * od, axis=-1, keepdims=True) + EPS) * ng_ref[...] * (1.0 - lambda_init)
        o_ref[:, heads[h]] = (od * _sigmoid(m_ref[:, heads[h]].astype(F32))).astype(BF16)


def _diff(l, lam, proj, ng, *, bsz, seq, tq, qs, hp, lambda_init, blk):
    vd = ng.shape[2]
    hd = vd // 2
    tq = min(tq, seq)
    nq = seq // tq
    w = hp * vd
    kern = functools.partial(_diff_kernel, tq=tq, qs=qs, hp=hp, hd=hd, lambda_init=lambda_init)
    return pl.pallas_call(
        kern,
        grid=(bsz, DIFF_HEADS // hp, nq),
        in_specs=[
            pl.BlockSpec(memory_space=pltpu.SMEM),
            pl.BlockSpec((tq, w), lambda b, h, i: (b * nq + i, blk["d_q"] + h)),
            pl.BlockSpec((seq, w), lambda b, h, i: (b, blk["d_k"] + h)),
            pl.BlockSpec((seq, w), lambda b, h, i: (b, blk["d_v"] + h)),
            pl.BlockSpec((tq, w), lambda b, h, i: (b * nq + i, blk["m_diff"] + h)),
            _layer(l, 1, vd),
        ],
        out_specs=pl.BlockSpec((tq, w), lambda b, h, i: (b * nq + i, h)),
        out_shape=jax.ShapeDtypeStruct((bsz * seq, DIFF_HEADS * vd), BF16),
        scratch_shapes=[pltpu.VMEM((hp, vd + ONES_ROWS, seq), BF16), pltpu.VMEM((hp, 2 * tq, vd), BF16),
                        pltpu.VMEM((hp, 1, 2 * tq), F32), pltpu.VMEM((hp, vd + ONES_ROWS, 2 * tq), F32),
                        pltpu.VMEM((hp * 2 * tq // qs, tq, qs), F32)],
        compiler_params=_cparams(("parallel", "parallel", "arbitrary")),
        name="diff",
    )(lam, proj, proj, proj, proj, ng)


def _lambda_kernel(p_ref, o_ref, *, lambda_init):
    p = p_ref[...]
    s1 = jnp.sum(p[0:1, :] * p[1:2, :], axis=-1, keepdims=True)
    s2 = jnp.sum(p[2:3, :] * p[3:4, :], axis=-1, keepdims=True)
    o_ref[...] = jnp.exp(s1) - jnp.exp(s2) + lambda_init


def _diff_lambda(l, p, lambda_init):
    out = pl.pallas_call(
        functools.partial(_lambda_kernel, lambda_init=lambda_init),
        grid=(1,),
        in_specs=[_layer(l, *p.shape[1:])],
        out_specs=pl.BlockSpec((1, 1), lambda i: (0, 0)),
        out_shape=jax.ShapeDtypeStruct((1, 1), F32),
        name="diff_lambda",
    )(p)
    return out.reshape(1)


def _mix_ffn_kernel(x_ref, yc_ref, yg_ref, yd_ref, wo_ref, g_ref, wi_ref, wd_ref, fg_ref, o_ref, *, th, final):
    hidden = wd_ref.shape[0]
    y = (yc_ref[...].astype(F32) + yg_ref[...].astype(F32) + yd_ref[...].astype(F32)).astype(BF16)
    x1 = x_ref[...] + jnp.dot(y, wo_ref[...], preferred_element_type=F32)
    ms = jnp.mean(x1 * x1, axis=-1, keepdims=True)
    h = (x1 * lax.rsqrt(ms + EPS) * g_ref[...]).astype(BF16)
    x2 = x1
    for c in range(hidden // th):
        gate = jnp.dot(h, wi_ref[:, c * th:(c + 1) * th], preferred_element_type=F32)
        up = jnp.dot(h, wi_ref[:, hidden + c * th:hidden + (c + 1) * th], preferred_element_type=F32)
        act = (gate * _sigmoid(gate) * up).astype(BF16)
        x2 = x2 + jnp.dot(act, wd_ref[c * th:(c + 1) * th, :], preferred_element_type=F32)
    if final:
        ms = jnp.mean(x2 * x2, axis=-1, keepdims=True)
        x2 = x2 * lax.rsqrt(ms + EPS) * fg_ref[...]
    o_ref[...] = x2


def _mix_ffn(l, x2, yc, yg, yd, w_out, g, w_ffn_in, w_ffn_out, fg, *, tm, th, final):
    t, d = x2.shape
    tm = min(tm, t)
    kern = functools.partial(_mix_ffn_kernel, th=th, final=final)

    def tok():
        return pl.BlockSpec((tm, d), lambda i: (i, 0))

    def resident(w):
        if w.ndim == 3:
            return pl.BlockSpec((None,) + w.shape[1:], lambda i: (l, 0, 0), pipeline_mode=pl.Buffered(1))
        return pl.BlockSpec(w.shape, lambda i: (0, 0), pipeline_mode=pl.Buffered(1))

    return pl.pallas_call(
        kern,
        grid=(t // tm,),
        in_specs=[tok(), tok(), tok(), tok(), resident(w_out), resident(g),
                  resident(w_ffn_in), resident(w_ffn_out), resident(fg)],
        out_specs=pl.BlockSpec((tm, d), lambda i: (i, 0)),
        out_shape=jax.ShapeDtypeStruct((t, d), F32),
        compiler_params=_cparams(("parallel",)),
        name="mix_ffn",
    )(x2, yc, yg, yd, w_out, g, w_ffn_in, w_ffn_out, fg)


def _regroup_in_proj(w_in, offs):
    lo, hi = offs["g_f"]
    w_main = jnp.concatenate([w_in[..., :lo], w_in[..., hi:]], axis=-1).astype(BF16)
    w_f = jnp.pad(w_in[..., lo:hi], ((0, 0), (0, 0), (0, LANES - (hi - lo)))).astype(BF16)
    return w_main, w_f


def kernel(x, norm_mix_g, w_in, conv_dw, conv_ln_g, conv_ln_b, w_conv_out, gla_wf_up, gla_bf, gla_norm_g,
           diff_lq1, diff_lk1, diff_lq2, diff_lk2, diff_norm_g, w_out, norm_ffn_g, w_ffn_in, w_ffn_out,
           final_norm_g):
    bsz, seq, d = x.shape
    depth = w_in.shape[0]
    rank, hdk = gla_wf_up.shape[1:]
    vd = diff_norm_g.shape[1]
    t = bsz * seq

    names = ("c_val", "c_gate", "g_q", "g_k", "g_v", "g_og", "g_f", "d_q", "d_k", "d_v", "m_conv", "m_gla", "m_diff")
    widths = (d, d, hdk, hdk, d, d, rank, d, d, d, d, d, d)
    offs, start = {}, 0
    for n, wd in zip(names, widths):
        offs[n] = (start, start + wd)
        start += wd
    hp = TILES["diff"]["hp"]
    unit = {"g_q": hdk, "g_k": hdk, "d_q": hp * vd, "d_k": hp * vd, "d_v": hp * vd, "m_diff": hp * vd}
    blk = {}
    for n in names:
        if n != "g_f":
            col = offs[n][0] - (rank if offs[n][0] > offs["g_f"][0] else 0)
            assert col % unit.get(n, d) == 0, n
            blk[n] = col // unit.get(n, d)

    w_main, w_f = _regroup_in_proj(w_in, offs)
    wf_up = jnp.pad(gla_wf_up, ((0, 0), (0, LANES - rank), (0, 0))).astype(BF16)
    w_pw, w_o, w_fi, w_fo = (w.astype(BF16) for w in (w_conv_out, w_out, w_ffn_in, w_ffn_out))
    row = lambda p: p[:, None, :]
    lam_p = jnp.stack([diff_lq1, diff_lk1, diff_lq2, diff_lk2], axis=1).astype(F32)

    x2 = x.reshape(t, d)
    for l in range(depth):
        lambda_init = 0.8 - 0.6 * math.exp(-0.3 * l)
        proj, f_low = _inproj(l, x2, row(norm_mix_g), w_main, w_f, **TILES["inproj"])
        y_conv = _conv(l, proj, conv_dw, row(conv_ln_g), row(conv_ln_b), w_pw, bsz=bsz, seq=seq, blk=blk,
                       **TILES["conv"])
        y_gla = _gla(l, proj, f_low, wf_up, row(gla_bf), row(gla_norm_g), bsz=bsz, seq=seq, blk=blk,
                     **TILES["gla"])
        lam = _diff_lambda(l, lam_p, lambda_init)
        y_diff = _diff(l, lam, proj, row(diff_norm_g), bsz=bsz, seq=seq, lambda_init=lambda_init, blk=blk,
                       **TILES["diff"])
        x2 = _mix_ffn(l, x2, y_conv, y_gla, y_diff, w_o, row(norm_ffn_g), w_fi, w_fo, final_norm_g[None, :],
                      final=(l == depth - 1), **TILES["mix_ffn"])
    return x2.reshape(bsz, seq, d)
```
